```python
import jax
import jax.numpy as jnp
from jax import lax
import numpy as np

D_MODEL = 2048
BATCH = 2
SEQ = 16384
DEPTH = 2

CTX_LEN = 256
GRID_W = 64
CONV_WIDTH = D_MODEL // 2
CONV_A_K = 3
RNN_WIDTH = D_MODEL
RNN_BLOCKS = 8
RNN_BLOCK = RNN_WIDTH // RNN_BLOCKS
CONV_B_K = 4
LRU_C = 8.0
PEER_HEADS = 8
PEER_KEYS = 128
PEER_EXPERTS = PEER_KEYS * PEER_KEYS
PEER_QDIM = 256
PEER_TOPK = 16
PEER_CHUNK = 128
EPS = 1e-6
RNN_X_COL = 3 * CONV_WIDTH
COL_SPLITS = (CONV_WIDTH, 2 * CONV_WIDTH, 3 * CONV_WIDTH, 3 * CONV_WIDTH + RNN_WIDTH,
              3 * CONV_WIDTH + 2 * RNN_WIDTH, 3 * CONV_WIDTH + 2 * RNN_WIDTH + D_MODEL)
IN_COLS = 3 * CONV_WIDTH + 2 * RNN_WIDTH + 2 * D_MODEL

kernel_name = 'hybrid_conv_rglru_peer_prefix_dit'


def rmsnorm(x, g):
    xf = x.astype(jnp.float32)
    y = xf * lax.rsqrt(jnp.mean(xf * xf, axis=-1, keepdims=True) + EPS)
    return (y * g.astype(jnp.float32)).astype(x.dtype)


def modulate(h, shift, scale):
    return h * (1 + scale) + shift


def adaln(cond, w, b):
    m = jax.nn.silu(cond) @ w + b
    return jnp.split(m, 6, axis=-1)


def dwconv(x, w, b, pad):
    y = lax.conv_general_dilated(x, w[:, None, :].astype(x.dtype), window_strides=(1,),
                                 padding=[pad], dimension_numbers=('NWC', 'WIO', 'NWC'),
                                 feature_group_count=x.shape[-1])
    return y + b


def seq_conv(x, w, b, pad, rows):
    if rows is None:
        return dwconv(x, w, b, pad)
    bsz, t, ch = x.shape
    y = dwconv(x.reshape(bsz * rows, GRID_W, ch), w, b, pad)
    return y.reshape(bsz, t, ch)


def linear_scan(a, u, h0, reverse):
    idx = -1 if reverse else 0
    u = u.at[:, idx].add(a[:, idx] * h0)

    def combine(left, right):
        al, ul = left
        ar, ur = right
        return al * ar, ar * ul + ur

    _, h = lax.associative_scan(combine, (a, u), axis=1, reverse=reverse)
    return h


def rglru(xr, conv_w, conv_b, wa, ba, wx, bx, lam, h0, rows):
    bsz, t, _ = xr.shape
    xc = seq_conv(xr, conv_w, conv_b, (2, 1), rows)
    xb = xc.reshape(bsz, t, RNN_BLOCKS, RNN_BLOCK)
    r = jax.nn.sigmoid(jnp.einsum('btnd,zndk->zbtnk', xb, wa).reshape(2, bsz, t, RNN_WIDTH) + ba[:, None, None, :])
    i = jax.nn.sigmoid(jnp.einsum('btnd,zndk->zbtnk', xb, wx).reshape(2, bsz, t, RNN_WIDTH) + bx[:, None, None, :])
    log_a = -LRU_C * r.astype(jnp.float32) * jax.nn.softplus(-lam.astype(jnp.float32))[:, None, None, :]
    a = jnp.exp(log_a)
    u = jnp.sqrt(-jnp.expm1(2.0 * log_a)) * (i * xc[None]).astype(jnp.float32)
    hf = linear_scan(a[0], u[0], h0[0], False)
    hb = linear_scan(a[1], u[1], h0[1], True)
    return hf, hb


def mixer(h, h0, rows, w_in, conv_a_w, conv_a_b, conv_b_w, conv_b_b, wa, ba, wx, bx, lam, w_pa, w_pb, w_o):
    proj = h @ w_in
    b_a, c_a, v_a, x_r, g_r, gate_a, gate_b = jnp.split(proj, COL_SPLITS, axis=-1)
    y_a = (b_a * seq_conv(c_a * v_a, conv_a_w, conv_a_b, (1, 1), rows)) @ w_pa
    hf, hb = rglru(x_r, conv_b_w, conv_b_b, wa, ba, wx, bx, lam, h0, rows)
    y_b = ((hf + hb).astype(h.dtype) * jax.nn.gelu(g_r)) @ w_pb
    merged = jax.nn.sigmoid(gate_a) * y_a + jax.nn.sigmoid(gate_b) * y_b
    final_state = jnp.stack([hf[:, -1], hb[:, 0]])
    return merged @ w_o, final_state


def peer(h, wq, k1, k2, u_tab, v_tab):
    bsz, t, d = h.shape
    blocks = h.reshape(-1, PEER_CHUNK, d)
    half = PEER_QDIM // 2

    def block_fn(xc):
        q = (xc @ wq).astype(jnp.float32).reshape(PEER_CHUNK, PEER_HEADS, PEER_QDIM)
        s1 = jnp.einsum('chd,hkd->chk', q[..., :half], k1.astype(jnp.float32))
        s2 = jnp.einsum('chd,hkd->chk', q[..., half:], k2.astype(jnp.float32))
        v1, i1 = lax.top_k(s1, PEER_TOPK)
        v2, i2 = lax.top_k(s2, PEER_TOPK)
        cand = (v1[..., :, None] + v2[..., None, :]).reshape(PEER_CHUNK, PEER_HEADS, PEER_TOPK * PEER_TOPK)
        score, j = lax.top_k(cand, PEER_TOPK)
        e = (jnp.take_along_axis(i1, j // PEER_TOPK, axis=-1) * PEER_KEYS
             + jnp.take_along_axis(i2, j % PEER_TOPK, axis=-1))
        w = jax.nn.softmax(score, axis=-1)
        act = jax.nn.gelu(jnp.einsum('cd,chkd->chk', xc, u_tab[e]).astype(jnp.float32))
        return jnp.einsum('chk,chkd->cd', (w * act).astype(xc.dtype), v_tab[e])

    return lax.map(block_fn, blocks).reshape(bsz, t, d)


def setup_inputs(seed: int = 0) -> dict:
    key = jax.random.key(seed)
    ks = jax.random.split(key, 32)
    f32 = jnp.float32
    L, D = DEPTH, D_MODEL

    def nrm(k, shape, scale):
        return jax.random.normal(k, shape, f32) * scale

    a0 = jax.random.uniform(ks[20], (L, 2, RNN_WIDTH), f32, 0.9, 0.999)
    p = a0 ** (1.0 / LRU_C)
    lam = jnp.log(p) - jnp.log1p(-p)
    return {
        'x': nrm(ks[0], (BATCH, SEQ, D), 1.0),
        'c': nrm(ks[1], (BATCH, D), 1.0),
        'ctx': nrm(ks[2], (BATCH, CTX_LEN, D), 1.0),
        'c_ctx': nrm(ks[3], (D,), 1.0),
        'ada_w': nrm(ks[4], (L, D, 6 * D), D ** -0.5),
        'ada_b': nrm(ks[5], (L, 6 * D), 0.02),
        'norm1_g': 1.0 + nrm(ks[6], (L, D), 0.02),
        'norm2_g': 1.0 + nrm(ks[7], (L, D), 0.02),
        'w_in': nrm(ks[8], (L, D, IN_COLS), D ** -0.5),
        'conv_a_w': nrm(ks[9], (L, CONV_A_K, CONV_WIDTH), CONV_A_K ** -0.5),
        'conv_a_b': nrm(ks[10], (L, CONV_WIDTH), 0.02),
        'conv_b_w': nrm(ks[11], (L, CONV_B_K, RNN_WIDTH), CONV_B_K ** -0.5),
        'conv_b_b': nrm(ks[12], (L, RNN_WIDTH), 0.02),
        'lru_wa': nrm(ks[13], (L, 2, RNN_BLOCKS, RNN_BLOCK, RNN_BLOCK), RNN_BLOCK ** -0.5),
        'lru_ba': nrm(ks[14], (L, 2, RNN_WIDTH), 0.02),
        'lru_wx': nrm(ks[15], (L, 2, RNN_BLOCKS, RNN_BLOCK, RNN_BLOCK), RNN_BLOCK ** -0.5),
        'lru_bx': nrm(ks[16], (L, 2, RNN_WIDTH), 0.02),
        'lru_lam': lam,
        'w_pa': nrm(ks[17], (L, CONV_WIDTH, D), CONV_WIDTH ** -0.5),
        'w_pb': nrm(ks[18], (L, RNN_WIDTH, D), RNN_WIDTH ** -0.5),
        'w_o': nrm(ks[19], (L, D, D), D ** -0.5),
        'peer_wq': nrm(ks[21], (L, D, PEER_HEADS * PEER_QDIM), D ** -0.5),
        'peer_k1': nrm(ks[22], (L, PEER_HEADS, PEER_KEYS, PEER_QDIM // 2), (PEER_QDIM // 2) ** -0.5),
        'peer_k2': nrm(ks[23], (L, PEER_HEADS, PEER_KEYS, PEER_QDIM // 2), (PEER_QDIM // 2) ** -0.5),
        'peer_u': nrm(ks[24], (L, PEER_EXPERTS, D), D ** -0.5),
        'peer_v': nrm(ks[25], (L, PEER_EXPERTS, D), 1.0),
        'final_g': 1.0 + nrm(ks[26], (D,), 0.02),
    }


def reference(x, c, ctx, c_ctx, ada_w, ada_b, norm1_g, norm2_g, w_in, conv_a_w, conv_a_b,
              conv_b_w, conv_b_b, lru_wa, lru_ba, lru_wx, lru_bx, lru_lam, w_pa, w_pb, w_o,
              peer_wq, peer_k1, peer_k2, peer_u, peer_v, final_g):
    bsz, seq, _ = x.shape
    rows = seq // GRID_W
    for l in range(DEPTH):
        sh1, sc1, g1, sh2, sc2, g2 = [m[:, None, :] for m in adaln(c, ada_w[l], ada_b[l])]
        csh1, csc1, cg1, csh2, csc2, cg2 = adaln(c_ctx, ada_w[l], ada_b[l])
        rnn = (conv_b_w[l], conv_b_b[l], lru_wa[l], lru_ba[l], lru_wx[l], lru_bx[l], lru_lam[l])
        mix = (w_in[l], conv_a_w[l], conv_a_b[l]) + rnn + (w_pa[l], w_pb[l], w_o[l])
        ffn = (peer_wq[l], peer_k1[l], peer_k2[l], peer_u[l], peer_v[l])
        zero_state = jnp.zeros((2, bsz, RNN_WIDTH), jnp.float32)
        hc = modulate(rmsnorm(ctx, norm1_g[l]), csh1, csc1)
        if l < DEPTH - 1:
            yc, ctx_state = mixer(hc, zero_state, None, *mix)
            ctx_next = ctx + cg1 * yc
            ctx_next = ctx_next + cg2 * peer(modulate(rmsnorm(ctx_next, norm2_g[l]), csh2, csc2), *ffn)
        else:
            x_r = hc @ w_in[l][:, RNN_X_COL:RNN_X_COL + RNN_WIDTH]
            hf_c, hb_c = rglru(x_r, *rnn, zero_state, None)
            ctx_state = jnp.stack([hf_c[:, -1], hb_c[:, 0]])
        h = modulate(rmsnorm(x, norm1_g[l]), sh1, sc1)
        y, _ = mixer(h, ctx_state, rows, *mix)
        x = x + g1 * y
        x = x + g2 * peer(modulate(rmsnorm(x, norm2_g[l]), sh2, sc2), *ffn)
        if l < DEPTH - 1:
            ctx = ctx_next
    return rmsnorm(x, final_g)
```

```python
import functools

import jax
import jax.numpy as jnp
from jax import lax
from jax.experimental import pallas as pl
from jax.experimental.pallas import tpu as pltpu

GRID_W = 64
TOPK = 16
LRU_C = 8.0
EPS = 1e-6
VMEM_LIMIT_BYTES = 56 * 1024 * 1024
LANES = 128

F32 = jnp.float32
BF16 = jnp.bfloat16
U32 = jnp.uint32
I32 = jnp.int32


def _params(*sem):
    return pltpu.CompilerParams(dimension_semantics=sem, vmem_limit_bytes=VMEM_LIMIT_BYTES)


def _resident(shape):
    nd = len(shape)
    return pl.BlockSpec(shape, lambda *_: (0,) * nd, pipeline_mode=pl.Buffered(1))


def _tile(n, target):
    if n <= target:
        return n
    t = target - target % LANES
    while n % t:
        t -= LANES
    return t


def _gelu(x):
    return 0.5 * x * (1.0 + jnp.tanh(0.7978845608028654 * (x + 0.044715 * x * x * x)))


def _sigmoid(x):
    return 1.0 / (1.0 + jnp.exp(-x))


def _adaln_kernel(cond_ref, w_ref, b_ref, o_ref):
    cnd = cond_ref[...]
    s = (cnd * _sigmoid(cnd)).astype(BF16)
    o_ref[...] = jnp.dot(s, w_ref[...].astype(BF16), preferred_element_type=F32) + b_ref[...]


def _adaln(cond, w, b):
    rows, d = cond.shape
    n = w.shape[1]
    tn = _tile(n, 1024)
    return pl.pallas_call(
        _adaln_kernel,
        grid=(n // tn,),
        in_specs=[pl.BlockSpec((rows, d), lambda j: (0, 0)),
                  pl.BlockSpec((d, tn), lambda j: (0, j)),
                  pl.BlockSpec((1, tn), lambda j: (0, j))],
        out_specs=pl.BlockSpec((rows, tn), lambda j: (0, j)),
        out_shape=jax.ShapeDtypeStruct((rows, n), F32),
        compiler_params=_params("arbitrary"),
        name="adaln",
    )(cond, w, b.reshape(1, n))


def _nmm_kernel(x_ref, g_ref, sh_ref, sc_ref, w_ref, o_ref, *rest, emit_h):
    if emit_h:
        h_out_ref, h_scr = rest
    else:
        (h_scr,) = rest

    @pl.when(pl.program_id(2) == 0)
    def _():
        x = x_ref[0]
        ms = jnp.mean(x * x, axis=-1, keepdims=True)
        y = x * lax.rsqrt(ms + EPS) * g_ref[...]
        h = y * (1.0 + sc_ref[0]) + sh_ref[0]
        h_scr[...] = h.astype(BF16)
        if emit_h:
            h_out_ref[0] = h

    o_ref[0] = jnp.dot(h_scr[...], w_ref[...], preferred_element_type=F32)


def _norm_mod_matmul(x, g, shift, scale, w, *, emit_h=False):
    bsz, t, d = x.shape
    n = w.shape[1]
    tm = min(512, t)
    tn = _tile(n, 1024)
    out_shape = [jax.ShapeDtypeStruct((bsz, t, n), F32)]
    out_specs = [pl.BlockSpec((1, tm, tn), lambda b, i, j: (b, i, j))]
    if emit_h:
        out_shape.append(jax.ShapeDtypeStruct((bsz, t, d), F32))
        out_specs.append(pl.BlockSpec((1, tm, d), lambda b, i, j: (b, i, 0)))
    res = pl.pallas_call(
        functools.partial(_nmm_kernel, emit_h=emit_h),
        grid=(bsz, t // tm, n // tn),
        in_specs=[pl.BlockSpec((1, tm, d), lambda b, i, j: (b, i, 0)),
                  pl.BlockSpec((1, d), lambda b, i, j: (0, 0)),
                  pl.BlockSpec((1, 1, d), lambda b, i, j: (b, 0, 0)),
                  pl.BlockSpec((1, 1, d), lambda b, i, j: (b, 0, 0)),
                  pl.BlockSpec((d, tn), lambda b, i, j: (0, j))],
        out_specs=out_specs,
        out_shape=out_shape,
        scratch_shapes=[pltpu.VMEM((tm, d), BF16)],
        compiler_params=_params("arbitrary", "arbitrary", "arbitrary"),
        name="norm_mod_matmul",
    )(x, g.reshape(1, d), shift, scale, w)
    return res if emit_h else res[0]


def _row_shift(x, off, pos, width):
    if off == 0:
        return x
    n = x.shape[0]
    r = pltpu.roll(x, (-off) % n, axis=0)
    valid = jnp.logical_and(pos + off >= 0, pos + off < width)
    return jnp.where(valid, r, 0.0)


def _scan_kernel(*refs, reverse, width, final, nblk):
    if final:
        (xr_ref, cw_ref, cb_ref, w_ref, ba_ref, bx_ref, lam_ref, h0_ref, hf_ref, gr_ref,
         o_ref, st_ref, carry) = refs
    else:
        (xr_ref, cw_ref, cb_ref, w_ref, ba_ref, bx_ref, lam_ref, h0_ref,
         o_ref, st_ref, carry) = refs

    @pl.when(pl.program_id(1) == 0)
    def _():
        carry[...] = h0_ref[0]

    x = xr_ref[0]
    tm, rw = x.shape
    bs = rw // nblk
    row = lax.broadcasted_iota(I32, (tm, 1), 0)
    pos = row % width
    cw = cw_ref[...]
    xc = cb_ref[...] + sum(cw[k:k + 1, :] * _row_shift(x, k - 2, pos, width) for k in range(4))

    for n in range(nblk):
        cols = slice(n * bs, (n + 1) * bs)
        xn = xc[:, cols]
        pre = jnp.dot(xn.astype(BF16), w_ref[n], preferred_element_type=F32)
        r = _sigmoid(pre[:, :bs] + ba_ref[:, cols])
        ig = _sigmoid(pre[:, bs:] + bx_ref[:, cols])
        z = -lam_ref[:, cols]
        sp = jnp.maximum(z, 0.0) + jnp.log(1.0 + jnp.exp(-jnp.abs(z)))
        log_a = -LRU_C * r * sp
        a = jnp.exp(log_a)
        u = jnp.sqrt(1.0 - a * a) * (ig * xn)
        d = 1
        while d < tm:
            if reverse:
                a_s = pltpu.roll(a, tm - d, axis=0)
                u_s = pltpu.roll(u, tm - d, axis=0)
                m = row < tm - d
            else:
                a_s = pltpu.roll(a, d, axis=0)
                u_s = pltpu.roll(u, d, axis=0)
                m = row >= d
            u = jnp.where(m, u + a * u_s, u)
            a = jnp.where(m, a * a_s, a)
            d *= 2
        h = u + a * carry[:, cols]
        carry[:, cols] = h[0:1, :] if reverse else h[tm - 1:tm, :]
        if final:
            zb = (hf_ref[0, :, cols] + h) * _gelu(gr_ref[0, :, cols])
            o_ref[0, :, cols] = zb.astype(o_ref.dtype)
        else:
            o_ref[0, :, cols] = h
    st_ref[0] = carry[...]


def _lru_scan(proj, conv_w, conv_b, wcat, ba, bx, lam, h0, *, reverse, width, hf=None):
    bsz, t, _ = proj.shape
    rw = conv_w.shape[1]
    nblk = wcat.shape[0]
    final = hf is not None
    tm = min(256, t)
    nt = t // tm
    tix = (lambda i: nt - 1 - i) if reverse else (lambda i: i)
    vec = lambda a: a.reshape(1, rw)
    in_specs = [pl.BlockSpec((1, tm, rw), lambda b, i: (b, tix(i), 0)),
                _resident((4, rw)), _resident((1, rw)), _resident(wcat.shape),
                _resident((1, rw)), _resident((1, rw)), _resident((1, rw)),
                pl.BlockSpec((1, 1, rw), lambda b, i: (b, 0, 0))]
    args = [proj, conv_w, vec(conv_b), wcat, vec(ba), vec(bx), vec(lam), h0]
    if final:
        in_specs += [pl.BlockSpec((1, tm, rw), lambda b, i: (b, tix(i), 0)),
                     pl.BlockSpec((1, tm, rw), lambda b, i: (b, tix(i), 1))]
        args += [hf, proj]
    out, state = pl.pallas_call(
        functools.partial(_scan_kernel, reverse=reverse, width=width, final=final, nblk=nblk),
        grid=(bsz, nt),
        in_specs=in_specs,
        out_specs=[pl.BlockSpec((1, tm, rw), lambda b, i: (b, tix(i), 0)),
                   pl.BlockSpec((1, 1, rw), lambda b, i: (b, 0, 0))],
        out_shape=[jax.ShapeDtypeStruct((bsz, t, rw), BF16 if final else F32),
                   jax.ShapeDtypeStruct((bsz, 1, rw), F32)],
        scratch_shapes=[pltpu.VMEM((1, rw), F32)],
        compiler_params=_params("arbitrary", "arbitrary"),
        name="lru_scan_bwd" if reverse else "lru_scan_fwd",
    )(*args)
    return out, state


def _merge_kernel(ga_ref, gb_ref, ba_ref, ca_ref, va_ref, zb_ref, cw_ref, cb_ref, wpa_ref, wpb_ref,
                  o_ref, *, width):
    cv = ca_ref[0] * va_ref[0]
    tm = cv.shape[0]
    pos = lax.broadcasted_iota(I32, (tm, 1), 0) % width
    cw = cw_ref[...]
    conv = cb_ref[...] + sum(cw[k:k + 1, :] * _row_shift(cv, k - 1, pos, width) for k in range(3))
    za = (ba_ref[0] * conv).astype(BF16)
    ya = jnp.dot(za, wpa_ref[...], preferred_element_type=F32)
    yb = jnp.dot(zb_ref[0], wpb_ref[...], preferred_element_type=F32)
    o_ref[0] = (_sigmoid(ga_ref[0]) * ya + _sigmoid(gb_ref[0]) * yb).astype(o_ref.dtype)


def _merge(proj, zb, conv_w, conv_b, w_pa, w_pb, *, width):
    bsz, t, _ = proj.shape
    cwid, d = w_pa.shape
    rw = w_pb.shape[0]
    tm = min(256, t)
    ga_blk = (2 * rw) // d
    ca_blk = (2 * rw + 2 * d) // cwid
    pblk = lambda wdt, k: pl.BlockSpec((1, tm, wdt), lambda b, i: (b, i, k))
    return pl.pallas_call(
        functools.partial(_merge_kernel, width=width),
        grid=(bsz, t // tm),
        in_specs=[pblk(d, ga_blk), pblk(d, ga_blk + 1),
                  pblk(cwid, ca_blk), pblk(cwid, ca_blk + 1), pblk(cwid, ca_blk + 2),
                  pl.BlockSpec((1, tm, rw), lambda b, i: (b, i, 0)),
                  _resident((3, cwid)), _resident((1, cwid)),
                  _resident((cwid, d)), _resident((rw, d))],
        out_specs=pl.BlockSpec((1, tm, d), lambda b, i: (b, i, 0)),
        out_shape=jax.ShapeDtypeStruct((bsz, t, d), BF16),
        compiler_params=_params("arbitrary", "arbitrary"),
        name="merge",
    )(proj, proj, proj, proj, proj, zb, conv_w, conv_b.reshape(1, cwid), w_pa, w_pb)


def _proj_res_kernel(m_ref, w_ref, x_ref, g_ref, o_ref):
    y = jnp.dot(m_ref[0], w_ref[...], preferred_element_type=F32)
    o_ref[0] = x_ref[0] + g_ref[0] * y


def _proj_residual(m, w, x, gate):
    bsz, t, d = x.shape
    k = m.shape[2]
    tm = min(512, t)
    return pl.pallas_call(
        _proj_res_kernel,
        grid=(bsz, t // tm),
        in_specs=[pl.BlockSpec((1, tm, k), lambda b, i: (b, i, 0)),
                  _resident((k, d)),
                  pl.BlockSpec((1, tm, d), lambda b, i: (b, i, 0)),
                  pl.BlockSpec((1, 1, d), lambda b, i: (b, 0, 0))],
        out_specs=pl.BlockSpec((1, tm, d), lambda b, i: (b, i, 0)),
        out_shape=jax.ShapeDtypeStruct((bsz, t, d), F32),
        compiler_params=_params("arbitrary", "arbitrary"),
        name="proj_residual",
    )(m, w, x, gate)


def _top16(s, payload=None):
    n = s.shape[0]
    iota = lax.broadcasted_iota(I32, s.shape, 0)
    vals, picks = [], []
    for _ in range(TOPK):
        m = jnp.max(s, axis=0, keepdims=True)
        idx = jnp.min(jnp.where(s == m, iota, n), axis=0, keepdims=True)
        hit = iota == idx
        vals.append(m)
        if payload is None:
            picks.append(idx)
        else:
            picks.append(jnp.sum(jnp.where(hit, payload, 0), axis=0, keepdims=True))
        s = jnp.where(hit, -jnp.inf, s)
    return jnp.concatenate(vals, axis=0), jnp.concatenate(picks, axis=0)


def _route_kernel(q_ref, k1_ref, k2_ref, e_ref, w_ref, *, heads, keys):
    q = q_ref[0].astype(BF16)
    half = k1_ref.shape[2]
    dn = (((1,), (1,)), ((), ()))
    es, ws = [], []
    for h in range(heads):
        q1 = q[:, h * 2 * half: h * 2 * half + half]
        q2 = q[:, h * 2 * half + half: (h + 1) * 2 * half]
        s1 = lax.dot_general(k1_ref[h], q1, dn, preferred_element_type=F32)
        s2 = lax.dot_general(k2_ref[h], q2, dn, preferred_element_type=F32)
        v1, i1 = _top16(s1)
        v2, i2 = _top16(s2)
        cand = jnp.concatenate([v1[a:a + 1, :] + v2 for a in range(TOPK)], axis=0)
        ecand = jnp.concatenate([i1[a:a + 1, :] * keys + i2 for a in range(TOPK)], axis=0)
        score, e = _top16(cand, ecand)
        p = jnp.exp(score - score[0:1, :])
        ws.append(p / jnp.sum(p, axis=0, keepdims=True))
        es.append(e)
    e_ref[0] = jnp.concatenate(es, axis=0).T
    w_ref[0] = jnp.concatenate(ws, axis=0).T


def _route(q, k1, k2):
    bsz, t, hq = q.shape
    heads, keys, _ = k1.shape
    tb = min(256, t)
    nsel = heads * TOPK
    return pl.pallas_call(
        functools.partial(_route_kernel, heads=heads, keys=keys),
        grid=(bsz, t // tb),
        in_specs=[pl.BlockSpec((1, tb, hq), lambda b, i: (b, i, 0)),
                  _resident(k1.shape), _resident(k2.shape)],
        out_specs=[pl.BlockSpec((1, tb, nsel), lambda b, i: (b, i, 0)),
                   pl.BlockSpec((1, tb, nsel), lambda b, i: (b, i, 0))],
        out_shape=[jax.ShapeDtypeStruct((bsz, t, nsel), I32),
                   jax.ShapeDtypeStruct((bsz, t, nsel), F32)],
        compiler_params=_params("arbitrary", "arbitrary"),
        name="peer_route",
    )(q, k1, k2)


def _pack_kernel(u_ref, v_ref, o_ref):
    half = u_ref.shape[1] // 2

    def pack(x):
        bits = lax.bitcast_convert_type(x.astype(BF16).astype(F32), U32)
        return (bits[:, :half] >> 16) | (bits[:, half:] & jnp.uint32(0xFFFF0000))

    o_ref[:, :half] = pack(u_ref[...])
    o_ref[:, half:] = pack(v_ref[...])


def _pack_tables(u, v):
    e, d = u.shape
    te = 256
    return pl.pallas_call(
        _pack_kernel,
        grid=(e // te,),
        in_specs=[pl.BlockSpec((te, d), lambda i: (i, 0)), pl.BlockSpec((te, d), lambda i: (i, 0))],
        out_specs=pl.BlockSpec((te, d), lambda i: (i, 0)),
        out_shape=jax.ShapeDtypeStruct((e, d), U32),
        compiler_params=_params("arbitrary"),
        name="pack_tables",
    )(u, v)


def _unpack(words):
    lo = lax.bitcast_convert_type(words << 16, F32)
    hi = lax.bitcast_convert_type(words & jnp.uint32(0xFFFF0000), F32)
    return lo, hi


NSLOT = 3
LOOKAHEAD = NSLOT - 1


def _expert_kernel(e_ref, w_ref, h_ref, x_ref, g_ref, fg_ref, tab_ref, o_ref, buf, sems, *, final_norm):
    tb, nsel = w_ref.shape[1], w_ref.shape[2]
    d = h_ref.shape[2]
    half = d // 2

    def row_copy(t, j, slot):
        return pltpu.make_async_copy(tab_ref.at[pl.ds(e_ref[0, t, j], 1), :],
                                     buf.at[slot, pl.ds(j, 1), :], sems.at[slot])

    def issue(t, slot):
        for j in range(nsel):
            row_copy(t, j, slot).start()

    def drain(t, slot):
        for j in range(nsel):
            row_copy(t, j, slot).wait()

    eye = lax.broadcasted_iota(I32, (nsel, nsel), 0) == lax.broadcasted_iota(I32, (nsel, nsel), 1)

    def compute(t, slot):
        hrow = h_ref[0, pl.ds(t, 1), :]
        wcol = jnp.sum(jnp.where(eye, w_ref[0, pl.ds(t, 1), :], 0.0), axis=1, keepdims=True)
        ulo, uhi = _unpack(buf[slot, :, :half])
        act = jnp.sum(ulo * hrow[:, :half] + uhi * hrow[:, half:], axis=1, keepdims=True)
        coef = wcol * _gelu(act)
        vlo, vhi = _unpack(buf[slot, :, half:])
        olo = jnp.sum(coef * vlo, axis=0, keepdims=True)
        ohi = jnp.sum(coef * vhi, axis=0, keepdims=True)
        g = g_ref[0]
        o_ref[0, pl.ds(t, 1), :half] = x_ref[0, pl.ds(t, 1), :half] + g[:, :half] * olo
        o_ref[0, pl.ds(t, 1), half:] = x_ref[0, pl.ds(t, 1), half:] + g[:, half:] * ohi

    for t in range(LOOKAHEAD):
        issue(t, t)

    def body(t, c):
        slot = t % NSLOT
        drain(t, slot)
        issue(t + LOOKAHEAD, (t + LOOKAHEAD) % NSLOT)
        compute(t, slot)
        return c

    lax.fori_loop(0, tb - LOOKAHEAD, body, 0)
    for t in range(tb - LOOKAHEAD, tb):
        drain(t, t % NSLOT)
        compute(t, t % NSLOT)

    if final_norm:
        y = o_ref[0]
        ms = jnp.mean(y * y, axis=-1, keepdims=True)
        o_ref[0] = y * lax.rsqrt(ms + EPS) * fg_ref[...]


def _experts(e, w, h, x, gate, final_g, table, *, final_norm):
    bsz, t, d = x.shape
    nsel = e.shape[2]
    tb = min(64, t)
    tok = lambda wdt: pl.BlockSpec((1, tb, wdt), lambda b, i: (b, i, 0))
    return pl.pallas_call(
        functools.partial(_expert_kernel, final_norm=final_norm),
        grid=(bsz, t // tb),
        in_specs=[pl.BlockSpec((1, tb, nsel), lambda b, i: (b, i, 0), memory_space=pltpu.SMEM),
                  tok(nsel), tok(d), tok(d),
                  pl.BlockSpec((1, 1, d), lambda b, i: (b, 0, 0)),
                  pl.BlockSpec((1, d), lambda b, i: (0, 0)),
                  pl.BlockSpec(memory_space=pl.ANY)],
        out_specs=tok(d),
        out_shape=jax.ShapeDtypeStruct((bsz, t, d), F32),
        scratch_shapes=[pltpu.VMEM((NSLOT, nsel, d), U32), pltpu.SemaphoreType.DMA((NSLOT,))],
        compiler_params=_params("arbitrary", "arbitrary"),
        name="peer_experts",
    )(e, w, h, x, gate, final_g.reshape(1, d), table)


def _mixer(x, mods, width, h0, prm, *, need_output=True):
    sh1, sc1, g1 = mods
    proj = _norm_mod_matmul(x, prm["norm1_g"], sh1, sc1, prm["w_in"])
    rnn = (prm["conv_b_w"], prm["conv_b_b"])
    hf, st_f = _lru_scan(proj, *rnn, prm["wcat"][0], prm["ba"][0], prm["bx"][0], prm["lam"][0], h0[0],
                         reverse=False, width=width)
    if not need_output:
        _, st_b = _lru_scan(proj, *rnn, prm["wcat"][1], prm["ba"][1], prm["bx"][1], prm["lam"][1], h0[1],
                            reverse=True, width=width)
        return None, (st_f, st_b)
    zb, st_b = _lru_scan(proj, *rnn, prm["wcat"][1], prm["ba"][1], prm["bx"][1], prm["lam"][1], h0[1],
                         reverse=True, width=width, hf=hf)
    merged = _merge(proj, zb, prm["conv_a_w"], prm["conv_a_b"], prm["w_pa"], prm["w_pb"], width=width)
    return _proj_residual(merged, prm["w_o"], x, g1), (st_f, st_b)


def _peer(x, mods, prm, final_g, *, final_norm=False):
    sh2, sc2, g2 = mods
    q, h = _norm_mod_matmul(x, prm["norm2_g"], sh2, sc2, prm["wq"], emit_h=True)
    e, w = _route(q, prm["k1"], prm["k2"])
    return _experts(e, w, h, x, g2, final_g, prm["table"], final_norm=final_norm)


def kernel(x, c, ctx, c_ctx, ada_w, ada_b, norm1_g, norm2_g, w_in, conv_a_w, conv_a_b, conv_b_w, conv_b_b,
           lru_wa, lru_ba, lru_wx, lru_bx, lru_lam, w_pa, w_pb, w_o, peer_wq, peer_k1, peer_k2, peer_u,
           peer_v, final_g):
    depth = ada_w.shape[0]
    bsz, _, d = x.shape
    ctx_len = ctx.shape[1]
    cwid = conv_a_w.shape[2]
    rw = conv_b_w.shape[2]
    assert rw == d and (2 * rw + 2 * d) % cwid == 0 and bsz + 1 <= 8

    cond = jnp.zeros((8, d), F32).at[:bsz].set(c).at[bsz].set(c_ctx)
    zero_state = jnp.zeros((2, bsz, 1, rw), F32)
    s3 = 3 * cwid

    for l in range(depth):
        w_in_l = w_in[l]
        prm = dict(
            norm1_g=norm1_g[l], norm2_g=norm2_g[l],
            w_in=jnp.concatenate([w_in_l[:, s3:], w_in_l[:, :s3]], axis=1).astype(BF16),
            conv_a_w=conv_a_w[l], conv_a_b=conv_a_b[l], conv_b_w=conv_b_w[l], conv_b_b=conv_b_b[l],
            wcat=jnp.concatenate([lru_wa[l], lru_wx[l]], axis=-1).astype(BF16),
            ba=lru_ba[l], bx=lru_bx[l], lam=lru_lam[l],
            w_pa=w_pa[l].astype(BF16), w_pb=w_pb[l].astype(BF16), w_o=w_o[l].astype(BF16),
            wq=peer_wq[l].astype(BF16), k1=peer_k1[l].astype(BF16), k2=peer_k2[l].astype(BF16),
            table=_pack_tables(peer_u[l], peer_v[l]),
        )
        m = _adaln(cond, ada_w[l], ada_b[l])
        lat = [m[:bsz, k * d:(k + 1) * d].reshape(bsz, 1, d) for k in range(6)]
        cmod = [jnp.broadcast_to(m[bsz, k * d:(k + 1) * d].reshape(1, 1, d), (bsz, 1, d)) for k in range(6)]
        last = l == depth - 1

        ctx_mix, ctx_state = _mixer(ctx, cmod[0:3], ctx_len, zero_state, prm, need_output=not last)
        if not last:
            ctx = _peer(ctx_mix, cmod[3:6], prm, final_g)
        x = _mixer(x, lat[0:3], GRID_W, ctx_state, prm)[0]
        x = _peer(x, lat[3:6], prm, final_g, final_norm=last)
    return x
```

```python
import functools

import jax
import jax.numpy as jnp
from jax import lax
from jax.experimental import pallas as pl
from jax.experimental.pallas import tpu as pltpu

GRID_W = 64
TOPK = 16
LRU_C = 8.0
EPS = 1e-6
VMEM_LIMIT_BYTES = 56 * 1024 * 1024
LANES = 128

F32 = jnp.float32
BF16 = jnp.bfloat16
U32 = jnp.uint32
I32 = jnp.int32


def _params(*sem):
    return pltpu.CompilerParams(dimension_semantics=sem, vmem_limit_bytes=VMEM_LIMIT_BYTES)


def _resident(shape):
    nd = len(shape)
    return pl.BlockSpec(shape, lambda *_: (0,) * nd, pipeline_mode=pl.Buffered(1))


def _tile(n, target):
    if n <= target:
        return n
    t = target - target % LANES
    while n % t:
        t -= LANES
    return t


def _gelu(x):
    return 0.5 * x * (1.0 + jnp.tanh(0.7978845608028654 * (x + 0.044715 * x * x * x)))


def _sigmoid(x):
    return 1.0 / (1.0 + jnp.exp(-x))


def _adaln_kernel(cond_ref, w_ref, b_ref, o_ref):
    cnd = cond_ref[...]
    s = (cnd * _sigmoid(cnd)).astype(BF16)
    o_ref[...] = jnp.dot(s, w_ref[...].astype(BF16), preferred_element_type=F32) + b_ref[...]


def _adaln(cond, w, b):
    rows, d = cond.shape
    n = w.shape[1]
    tn = _tile(n, 1024)
    return pl.pallas_call(
        _adaln_kernel,
        grid=(n // tn,),
        in_specs=[pl.BlockSpec((rows, d), lambda j: (0, 0)),
                  pl.BlockSpec((d, tn), lambda j: (0, j)),
                  pl.BlockSpec((1, tn), lambda j: (0, j))],
        out_specs=pl.BlockSpec((rows, tn), lambda j: (0, j)),
        out_shape=jax.ShapeDtypeStruct((rows, n), F32),
        compiler_params=_params("arbitrary"),
        name="adaln",
    )(cond, w, b.reshape(1, n))


def _nmm_kernel(x_ref, g_ref, sh_ref, sc_ref, w_ref, o_ref, *rest, emit_h):
    if emit_h:
        h_out_ref, h_scr = rest
    else:
        (h_scr,) = rest

    @pl.when(pl.program_id(2) == 0)
    def _():
        x = x_ref[0]
        ms = jnp.mean(x * x, axis=-1, keepdims=True)
        y = x * lax.rsqrt(ms + EPS) * g_ref[...]
        h = y * (1.0 + sc_ref[0]) + sh_ref[0]
        h_scr[...] = h.astype(BF16)
        if emit_h:
            h_out_ref[0] = h

    o_ref[0] = jnp.dot(h_scr[...], w_ref[...], preferred_element_type=F32)


def _norm_mod_matmul(x, g, shift, scale, w, *, emit_h=False):
    bsz, t, d = x.shape
    n = w.shape[1]
    tm = min(512, t)
    tn = _tile(n, 1024)
    out_shape = [jax.ShapeDtypeStruct((bsz, t, n), F32)]
    out_specs = [pl.BlockSpec((1, tm, tn), lambda b, i, j: (b, i, j))]
    if emit_h:
        out_shape.append(jax.ShapeDtypeStruct((bsz, t, d), F32))
        out_specs.append(pl.BlockSpec((1, tm, d), lambda b, i, j: (b, i, 0)))
    res = pl.pallas_call(
        functools.partial(_nmm_kernel, emit_h=emit_h),
        grid=(bsz, t // tm, n // tn),
        in_specs=[pl.BlockSpec((1, tm, d), lambda b, i, j: (b, i, 0)),
                  pl.BlockSpec((1, d), lambda b, i, j: (0, 0)),
                  pl.BlockSpec((1, 1, d), lambda b, i, j: (b, 0, 0)),
                  pl.BlockSpec((1, 1, d), lambda b, i, j: (b, 0, 0)),
                  pl.BlockSpec((d, tn), lambda b, i, j: (0, j))],
        out_specs=out_specs,
        out_shape=out_shape,
        scratch_shapes=[pltpu.VMEM((tm, d), BF16)],
        compiler_params=_params("arbitrary", "arbitrary", "arbitrary"),
        name="norm_mod_matmul",
    )(x, g.reshape(1, d), shift, scale, w)
    return res if emit_h else res[0]


def _row_shift(x, off, pos, width):
    if off == 0:
        return x
    n = x.shape[0]
    r = pltpu.roll(x, (-off) % n, axis=0)
    valid = jnp.logical_and(pos + off >= 0, pos + off < width)
    return jnp.where(valid, r, 0.0)


def _scan_kernel(*refs, reverse, width, final, nblk):
    if final:
        (xr_ref, cw_ref, cb_ref, w_ref, ba_ref, bx_ref, lam_ref, h0_ref, hf_ref, gr_ref,
         o_ref, st_ref, carry) = refs
    else:
        (xr_ref, cw_ref, cb_ref, w_ref, ba_ref, bx_ref, lam_ref, h0_ref,
         o_ref, st_ref, carry) = refs

    @pl.when(pl.program_id(1) == 0)
    def _():
        carry[...] = h0_ref[0]

    x = xr_ref[0]
    tm, rw = x.shape
    bs = rw // nblk
    row = lax.broadcasted_iota(I32, (tm, 1), 0)
    pos = row % width
    cw = cw_ref[...]
    xc = cb_ref[...] + sum(cw[k:k + 1, :] * _row_shift(x, k - 2, pos, width) for k in range(4))

    for n in range(nblk):
        cols = slice(n * bs, (n + 1) * bs)
        xn = xc[:, cols]
        pre = jnp.dot(xn.astype(BF16), w_ref[n], preferred_element_type=F32)
        r = _sigmoid(pre[:, :bs] + ba_ref[:, cols])
        ig = _sigmoid(pre[:, bs:] + bx_ref[:, cols])
        z = -lam_ref[:, cols]
        sp = jnp.maximum(z, 0.0) + jnp.log(1.0 + jnp.exp(-jnp.abs(z)))
        log_a = -LRU_C * r * sp
        a = jnp.exp(log_a)
        u = jnp.sqrt(1.0 - a * a) * (ig * xn)
        d = 1
        while d < tm:
            if reverse:
                a_s = pltpu.roll(a, tm - d, axis=0)
                u_s = pltpu.roll(u, tm - d, axis=0)
                m = row < tm - d
            else:
                a_s = pltpu.roll(a, d, axis=0)
                u_s = pltpu.roll(u, d, axis=0)
                m = row >= d
            u = jnp.where(m, u + a * u_s, u)
            a = jnp.where(m, a * a_s, a)
            d *= 2
        h = u + a * carry[:, cols]
        carry[:, cols] = h[0:1, :] if reverse else h[tm - 1:tm, :]
        if final:
            zb = (hf_ref[0, :, cols] + h) * _gelu(gr_ref[0, :, cols])
            o_ref[0, :, cols] = zb.astype(o_ref.dtype)
        else:
            o_ref[0, :, cols] = h
    st_ref[0] = carry[...]


def _lru_scan(proj, conv_w, conv_b, wcat, ba, bx, lam, h0, *, reverse, width, hf=None):
    bsz, t, _ = proj.shape
    rw = conv_w.shape[1]
    nblk = wcat.shape[0]
    final = hf is not None
    tm = min(256, t)
    nt = t // tm
    tix = (lambda i: nt - 1 - i) if reverse else (lambda i: i)
    vec = lambda a: a.reshape(1, rw)
    in_specs = [pl.BlockSpec((1, tm, rw), lambda b, i: (b, tix(i), 0)),
                _resident((4, rw)), _resident((1, rw)), _resident(wcat.shape),
                _resident((1, rw)), _resident((1, rw)), _resident((1, rw)),
                pl.BlockSpec((1, 1, rw), lambda b, i: (b, 0, 0))]
    args = [proj, conv_w, vec(conv_b), wcat, vec(ba), vec(bx), vec(lam), h0]
    if final:
        in_specs += [pl.BlockSpec((1, tm, rw), lambda b, i: (b, tix(i), 0)),
                     pl.BlockSpec((1, tm, rw), lambda b, i: (b, tix(i), 1))]
        args += [hf, proj]
    out, state = pl.pallas_call(
        functools.partial(_scan_kernel, reverse=reverse, width=width, final=final, nblk=nblk),
        grid=(bsz, nt),
        in_specs=in_specs,
        out_specs=[pl.BlockSpec((1, tm, rw), lambda b, i: (b, tix(i), 0)),
                   pl.BlockSpec((1, 1, rw), lambda b, i: (b, 0, 0))],
        out_shape=[jax.ShapeDtypeStruct((bsz, t, rw), BF16 if final else F32),
                   jax.ShapeDtypeStruct((bsz, 1, rw), F32)],
        scratch_shapes=[pltpu.VMEM((1, rw), F32)],
        compiler_params=_params("arbitrary", "arbitrary"),
        name="lru_scan_bwd" if reverse else "lru_scan_fwd",
    )(*args)
    return out, state


def _merge_kernel(ga_ref, gb_ref, ba_ref, ca_ref, va_ref, zb_ref, cw_ref, cb_ref, wpa_ref, wpb_ref,
                  o_ref, *, width):
    cv = ca_ref[0] * va_ref[0]
    tm = cv.shape[0]
    pos = lax.broadcasted_iota(I32, (tm, 1), 0) % width
    cw = cw_ref[...]
    conv = cb_ref[...] + sum(cw[k:k + 1, :] * _row_shift(cv, k - 1, pos, width) for k in range(3))
    za = (ba_ref[0] * conv).astype(BF16)
    ya = jnp.dot(za, wpa_ref[...], preferred_element_type=F32)
    yb = jnp.dot(zb_ref[0], wpb_ref[...], preferred_element_type=F32)
    o_ref[0] = (_sigmoid(ga_ref[0]) * ya + _sigmoid(gb_ref[0]) * yb).astype(o_ref.dtype)


def _merge(proj, zb, conv_w, conv_b, w_pa, w_pb, *, width):
    bsz, t, _ = proj.shape
    cwid, d = w_pa.shape
    rw = w_pb.shape[0]
    tm = min(256, t)
    ga_blk = (2 * rw) // d
    ca_blk = (2 * rw + 2 * d) // cwid
    pblk = lambda wdt, k: pl.BlockSpec((1, tm, wdt), lambda b, i: (b, i, k))
    return pl.pallas_call(
        functools.partial(_merge_kernel, width=width),
        grid=(bsz, t // tm),
        in_specs=[pblk(d, ga_blk), pblk(d, ga_blk + 1),
                  pblk(cwid, ca_blk), pblk(cwid, ca_blk + 1), pblk(cwid, ca_blk + 2),
                  pl.BlockSpec((1, tm, rw), lambda b, i: (b, i, 0)),
                  _resident((3, cwid)), _resident((1, cwid)),
                  _resident((cwid, d)), _resident((rw, d))],
        out_specs=pl.BlockSpec((1, tm, d), lambda b, i: (b, i, 0)),
        out_shape=jax.ShapeDtypeStruct((bsz, t, d), BF16),
        compiler_params=_params("arbitrary", "arbitrary"),
        name="merge",
    )(proj, proj, proj, proj, proj, zb, conv_w, conv_b.reshape(1, cwid), w_pa, w_pb)


def _proj_res_kernel(m_ref, w_ref, x_ref, g_ref, o_ref):
    y = jnp.dot(m_ref[0], w_ref[...], preferred_element_type=F32)
    o_ref[0] = x_ref[0] + g_ref[0] * y


def _proj_residual(m, w, x, gate):
    bsz, t, d = x.shape
    k = m.shape[2]
    tm = min(512, t)
    return pl.pallas_call(
        _proj_res_kernel,
        grid=(bsz, t // tm),
        in_specs=[pl.BlockSpec((1, tm, k), lambda b, i: (b, i, 0)),
                  _resident((k, d)),
                  pl.BlockSpec((1, tm, d), lambda b, i: (b, i, 0)),
                  pl.BlockSpec((1, 1, d), lambda b, i: (b, 0, 0))],
        out_specs=pl.BlockSpec((1, tm, d), lambda b, i: (b, i, 0)),
        out_shape=jax.ShapeDtypeStruct((bsz, t, d), F32),
        compiler_params=_params("arbitrary", "arbitrary"),
        name="proj_residual",
    )(m, w, x, gate)


def _top16(s, payload=None):
    iota = lax.broadcasted_iota(I32, s.shape, 0).astype(F32)
    vals, picks = [], []
    for _ in range(TOPK):
        m = jnp.max(s, axis=0, keepdims=True)
        idx = jnp.min(jnp.where(s == m, iota, float(s.shape[0])), axis=0, keepdims=True)
        hit = iota == idx
        vals.append(m)
        if payload is None:
            picks.append(idx)
        else:
            picks.append(jnp.sum(jnp.where(hit, payload, 0.0), axis=0, keepdims=True))
        s = jnp.where(hit, -jnp.inf, s)
    return jnp.concatenate(vals, axis=0), jnp.concatenate(picks, axis=0)


def _pair_candidates(v1, i1, v2, i2, keys):
    sub = 8
    row = lax.broadcasted_iota(I32, (sub, 1), 0)
    cand = [v1[0:1] + v2]
    ecand = [i1[0:1] * keys + i2]
    for a in range(1, sub):
        ok = row < TOPK // (a + 1)
        cand.append(jnp.where(ok, v1[a:a + 1] + v2[0:sub], -jnp.inf))
        ecand.append(i1[a:a + 1] * keys + i2[0:sub])
    cand.append(v1[sub:] + v2[0:1])
    ecand.append(i1[sub:] * keys + i2[0:1])
    return jnp.concatenate(cand, axis=0), jnp.concatenate(ecand, axis=0)


def _route_kernel(q_ref, k1_ref, k2_ref, e_ref, w_ref, *, heads, keys):
    q = q_ref[0].astype(BF16)
    half = k1_ref.shape[2]
    dn = (((1,), (1,)), ((), ()))
    es, ws = [], []
    for h in range(heads):
        q1 = q[:, h * 2 * half: h * 2 * half + half]
        q2 = q[:, h * 2 * half + half: (h + 1) * 2 * half]
        s1 = lax.dot_general(k1_ref[h], q1, dn, preferred_element_type=F32)
        s2 = lax.dot_general(k2_ref[h], q2, dn, preferred_element_type=F32)
        v1, i1 = _top16(s1)
        v2, i2 = _top16(s2)
        score, e = _top16(*_pair_candidates(v1, i1, v2, i2, float(keys)))
        p = jnp.exp(score - score[0:1, :])
        ws.append(p / jnp.sum(p, axis=0, keepdims=True))
        es.append(e)
    e_ref[0] = jnp.concatenate(es, axis=0).T.astype(I32)
    w_ref[0] = jnp.concatenate(ws, axis=0).T


def _route(q, k1, k2):
    bsz, t, hq = q.shape
    heads, keys, _ = k1.shape
    tb = min(256, t)
    nsel = heads * TOPK
    return pl.pallas_call(
        functools.partial(_route_kernel, heads=heads, keys=keys),
        grid=(bsz, t // tb),
        in_specs=[pl.BlockSpec((1, tb, hq), lambda b, i: (b, i, 0)),
                  _resident(k1.shape), _resident(k2.shape)],
        out_specs=[pl.BlockSpec((1, tb, nsel), lambda b, i: (b, i, 0)),
                   pl.BlockSpec((1, tb, nsel), lambda b, i: (b, i, 0))],
        out_shape=[jax.ShapeDtypeStruct((bsz, t, nsel), I32),
                   jax.ShapeDtypeStruct((bsz, t, nsel), F32)],
        compiler_params=_params("arbitrary", "arbitrary"),
        name="peer_route",
    )(q, k1, k2)


def _pack_kernel(u_ref, v_ref, o_ref):
    half = u_ref.shape[1] // 2

    def pack(x):
        bits = lax.bitcast_convert_type(x.astype(BF16).astype(F32), U32)
        return (bits[:, :half] >> 16) | (bits[:, half:] & jnp.uint32(0xFFFF0000))

    o_ref[:, :half] = pack(u_ref[...])
    o_ref[:, half:] = pack(v_ref[...])


def _pack_tables(u, v):
    e, d = u.shape
    te = 256
    return pl.pallas_call(
        _pack_kernel,
        grid=(e // te,),
        in_specs=[pl.BlockSpec((te, d), lambda i: (i, 0)), pl.BlockSpec((te, d), lambda i: (i, 0))],
        out_specs=pl.BlockSpec((te, d), lambda i: (i, 0)),
        out_shape=jax.ShapeDtypeStruct((e, d), U32),
        compiler_params=_params("arbitrary"),
        name="pack_tables",
    )(u, v)


def _unpack(words):
    lo = lax.bitcast_convert_type(words << 16, F32)
    hi = lax.bitcast_convert_type(words & jnp.uint32(0xFFFF0000), F32)
    return lo, hi


NSLOT = 4
LOOKAHEAD = NSLOT - 1
SUBLANES = 8


def _expert_kernel(e_ref, w_ref, h_ref, x_ref, g_ref, fg_ref, tab_ref, o_ref, *scratch, final_norm):
    bufs, sems = scratch[:NSLOT], scratch[NSLOT]
    tb, nsel = w_ref.shape[1], w_ref.shape[2]
    d = h_ref.shape[2]
    half = d // 2
    ngrp = nsel // SUBLANES

    def row_copy(t, j, slot):
        return pltpu.make_async_copy(tab_ref.at[pl.ds(e_ref[0, t, j], 1), :],
                                     bufs[slot].at[pl.ds(j, 1), :], sems.at[slot])

    lane = lax.broadcasted_iota(I32, (SUBLANES, nsel), 1)
    sub = lax.broadcasted_iota(I32, (SUBLANES, nsel), 0)

    def token(t, slot, t_next):
        for j in range(nsel):
            row_copy(t, j, slot).wait()
        buf = bufs[slot]
        nslot = (slot + LOOKAHEAD) % NSLOT
        hrow = h_ref[0, pl.ds(t, 1), :]
        xlo = jnp.broadcast_to(hrow[:, :half], (SUBLANES, half))
        xhi = jnp.broadcast_to(hrow[:, half:], (SUBLANES, half))
        wrow = w_ref[0, pl.ds(t, 1), :]
        acc_lo = jnp.zeros((SUBLANES, half), F32)
        acc_hi = jnp.zeros((SUBLANES, half), F32)
        for g in range(ngrp):
            rows = slice(g * SUBLANES, (g + 1) * SUBLANES)
            ulo, uhi = _unpack(buf[rows, :half])
            act = jnp.sum(ulo * xlo + uhi * xhi, axis=1, keepdims=True)
            wcol = jnp.sum(jnp.where(lane == sub + g * SUBLANES, wrow, 0.0), axis=1, keepdims=True)
            coef = wcol * _gelu(act)
            vlo, vhi = _unpack(buf[rows, half:])
            acc_lo = acc_lo + coef * vlo
            acc_hi = acc_hi + coef * vhi
            if t_next is not None:
                for j in range(g * SUBLANES, (g + 1) * SUBLANES):
                    row_copy(t_next, j, nslot).start()
        gate = g_ref[0]
        olo = jnp.sum(acc_lo, axis=0, keepdims=True)
        ohi = jnp.sum(acc_hi, axis=0, keepdims=True)
        o_ref[0, pl.ds(t, 1), :half] = x_ref[0, pl.ds(t, 1), :half] + gate[:, :half] * olo
        o_ref[0, pl.ds(t, 1), half:] = x_ref[0, pl.ds(t, 1), half:] + gate[:, half:] * ohi

    for t in range(LOOKAHEAD):
        for j in range(nsel):
            row_copy(t, j, t).start()

    def body(q, c):
        for s in range(NSLOT):
            t = q * NSLOT + s
            token(t, s, t + LOOKAHEAD)
        return c

    lax.fori_loop(0, tb // NSLOT - 1, body, 0)
    for s in range(NSLOT):
        t = tb - NSLOT + s
        token(t, s, t + LOOKAHEAD if s == 0 else None)

    if final_norm:
        y = o_ref[0]
        ms = jnp.mean(y * y, axis=-1, keepdims=True)
        o_ref[0] = y * lax.rsqrt(ms + EPS) * fg_ref[...]


def _experts(e, w, h, x, gate, final_g, table, *, final_norm):
    bsz, t, d = x.shape
    nsel = e.shape[2]
    tb = min(256, t)
    assert tb % NSLOT == 0 and nsel % SUBLANES == 0
    tok = lambda wdt: pl.BlockSpec((1, tb, wdt), lambda b, i: (b, i, 0))
    return pl.pallas_call(
        functools.partial(_expert_kernel, final_norm=final_norm),
        grid=(bsz, t // tb),
        in_specs=[pl.BlockSpec((1, tb, nsel), lambda b, i: (b, i, 0), memory_space=pltpu.SMEM),
                  tok(nsel), tok(d), tok(d),
                  pl.BlockSpec((1, 1, d), lambda b, i: (b, 0, 0)),
                  pl.BlockSpec((1, d), lambda b, i: (0, 0)),
                  pl.BlockSpec(memory_space=pl.ANY)],
        out_specs=tok(d),
        out_shape=jax.ShapeDtypeStruct((bsz, t, d), F32),
        scratch_shapes=[pltpu.VMEM((nsel, d), U32)] * NSLOT + [pltpu.SemaphoreType.DMA((NSLOT,))],
        compiler_params=_params("arbitrary", "arbitrary"),
        name="peer_experts",
    )(e, w, h, x, gate, final_g.reshape(1, d), table)


def _mixer(x, mods, width, h0, prm, *, need_output=True):
    sh1, sc1, g1 = mods
    proj = _norm_mod_matmul(x, prm["norm1_g"], sh1, sc1, prm["w_in"])
    rnn = (prm["conv_b_w"], prm["conv_b_b"])
    hf, st_f = _lru_scan(proj, *rnn, prm["wcat"][0], prm["ba"][0], prm["bx"][0], prm["lam"][0], h0[0],
                         reverse=False, width=width)
    if not need_output:
        _, st_b = _lru_scan(proj, *rnn, prm["wcat"][1], prm["ba"][1], prm["bx"][1], prm["lam"][1], h0[1],
                            reverse=True, width=width)
        return None, (st_f, st_b)
    zb, st_b = _lru_scan(proj, *rnn, prm["wcat"][1], prm["ba"][1], prm["bx"][1], prm["lam"][1], h0[1],
                         reverse=True, width=width, hf=hf)
    merged = _merge(proj, zb, prm["conv_a_w"], prm["conv_a_b"], prm["w_pa"], prm["w_pb"], width=width)
    return _proj_residual(merged, prm["w_o"], x, g1), (st_f, st_b)


def _peer(x, mods, prm, final_g, *, final_norm=False):
    sh2, sc2, g2 = mods
    q, h = _norm_mod_matmul(x, prm["norm2_g"], sh2, sc2, prm["wq"], emit_h=True)
    e, w = _route(q, prm["k1"], prm["k2"])
    return _experts(e, w, h, x, g2, final_g, prm["table"], final_norm=final_norm)


def kernel(x, c, ctx, c_ctx, ada_w, ada_b, norm1_g, norm2_g, w_in, conv_a_w, conv_a_b, conv_b_w, conv_b_b,
           lru_wa, lru_ba, lru_wx, lru_bx, lru_lam, w_pa, w_pb, w_o, peer_wq, peer_k1, peer_k2, peer_u,
           peer_v, final_g):
    depth = ada_w.shape[0]
    bsz, _, d = x.shape
    ctx_len = ctx.shape[1]
    cwid = conv_a_w.shape[2]
    rw = conv_b_w.shape[2]
    assert rw == d and (2 * rw + 2 * d) % cwid == 0 and bsz + 1 <= 8

    cond = jnp.zeros((8, d), F32).at[:bsz].set(c).at[bsz].set(c_ctx)
    zero_state = jnp.zeros((2, bsz, 1, rw), F32)
    s3 = 3 * cwid

    for l in range(depth):
        w_in_l = w_in[l]
        prm = dict(
            norm1_g=norm1_g[l], norm2_g=norm2_g[l],
            w_in=jnp.concatenate([w_in_l[:, s3:], w_in_l[:, :s3]], axis=1).astype(BF16),
            conv_a_w=conv_a_w[l], conv_a_b=conv_a_b[l], conv_b_w=conv_b_w[l], conv_b_b=conv_b_b[l],
            wcat=jnp.concatenate([lru_wa[l], lru_wx[l]], axis=-1).astype(BF16),
            ba=lru_ba[l], bx=lru_bx[l], lam=lru_lam[l],
            w_pa=w_pa[l].astype(BF16), w_pb=w_pb[l].astype(BF16), w_o=w_o[l].astype(BF16),
            wq=peer_wq[l].astype(BF16), k1=peer_k1[l].astype(BF16), k2=peer_k2[l].astype(BF16),
            table=_pack_tables(peer_u[l], peer_v[l]),
        )
        m = _adaln(cond, ada_w[l], ada_b[l])
        lat = [m[:bsz, k * d:(k + 1) * d].reshape(bsz, 1, d) for k in range(6)]
        cmod = [jnp.broadcast_to(m[bsz, k * d:(k + 1) * d].reshape(1, 1, d), (bsz, 1, d)) for k in range(6)]
        last = l == depth - 1

        ctx_mix, ctx_state = _mixer(ctx, cmod[0:3], ctx_len, zero_state, prm, need_output=not last)
        if not last:
            ctx = _peer(ctx_mix, cmod[3:6], prm, final_g)
        x = _mixer(x, lat[0:3], GRID_W, ctx_state, prm)[0]
        x = _peer(x, lat[3:6], prm, final_g, final_norm=last)
    return x
```

```python
import functools

import jax
import jax.numpy as jnp
from jax import lax
from jax.experimental import pallas as pl
from jax.experimental.pallas import tpu as pltpu
from jax.experimental.pallas import tpu_sc as plsc

GRID_W = 64
TOPK = 16
LRU_C = 8.0
EPS = 1e-6
VMEM_LIMIT_BYTES = 56 * 1024 * 1024
LANES = 128

F32 = jnp.float32
BF16 = jnp.bfloat16
U32 = jnp.uint32
I32 = jnp.int32


def _params(*sem):
    return pltpu.CompilerParams(dimension_semantics=sem, vmem_limit_bytes=VMEM_LIMIT_BYTES)


def _resident(shape):
    nd = len(shape)
    return pl.BlockSpec(shape, lambda *_: (0,) * nd, pipeline_mode=pl.Buffered(1))


def _tile(n, target):
    if n <= target:
        return n
    t = target - target % LANES
    while n % t:
        t -= LANES
    return t


def _gelu(x):
    return 0.5 * x * (1.0 + jnp.tanh(0.7978845608028654 * (x + 0.044715 * x * x * x)))


def _sigmoid(x):
    return 1.0 / (1.0 + jnp.exp(-x))


def _adaln_kernel(cond_ref, w_ref, b_ref, o_ref):
    cnd = cond_ref[...]
    s = (cnd * _sigmoid(cnd)).astype(BF16)
    o_ref[...] = jnp.dot(s, w_ref[...].astype(BF16), preferred_element_type=F32) + b_ref[...]


def _adaln(cond, w, b):
    rows, d = cond.shape
    n = w.shape[1]
    tn = _tile(n, 1024)
    return pl.pallas_call(
        _adaln_kernel,
        grid=(n // tn,),
        in_specs=[pl.BlockSpec((rows, d), lambda j: (0, 0)),
                  pl.BlockSpec((d, tn), lambda j: (0, j)),
                  pl.BlockSpec((1, tn), lambda j: (0, j))],
        out_specs=pl.BlockSpec((rows, tn), lambda j: (0, j)),
        out_shape=jax.ShapeDtypeStruct((rows, n), F32),
        compiler_params=_params("arbitrary"),
        name="adaln",
    )(cond, w, b.reshape(1, n))


def _nmm_kernel(x_ref, g_ref, sh_ref, sc_ref, w_ref, o_ref, *rest, emit_h):
    if emit_h:
        h_out_ref, h_scr = rest
    else:
        (h_scr,) = rest

    @pl.when(pl.program_id(2) == 0)
    def _():
        x = x_ref[0]
        ms = jnp.mean(x * x, axis=-1, keepdims=True)
        y = x * lax.rsqrt(ms + EPS) * g_ref[...]
        h = y * (1.0 + sc_ref[0]) + sh_ref[0]
        h_scr[...] = h.astype(BF16)
        if emit_h:
            h_out_ref[0] = h

    o_ref[0] = jnp.dot(h_scr[...], w_ref[...], preferred_element_type=F32)


def _norm_mod_matmul(x, g, shift, scale, w, *, emit_h=False):
    bsz, t, d = x.shape
    n = w.shape[1]
    tm = min(512, t)
    tn = _tile(n, 1024)
    out_shape = [jax.ShapeDtypeStruct((bsz, t, n), F32)]
    out_specs = [pl.BlockSpec((1, tm, tn), lambda b, i, j: (b, i, j))]
    if emit_h:
        out_shape.append(jax.ShapeDtypeStruct((bsz, t, d), F32))
        out_specs.append(pl.BlockSpec((1, tm, d), lambda b, i, j: (b, i, 0)))
    res = pl.pallas_call(
        functools.partial(_nmm_kernel, emit_h=emit_h),
        grid=(bsz, t // tm, n // tn),
        in_specs=[pl.BlockSpec((1, tm, d), lambda b, i, j: (b, i, 0)),
                  pl.BlockSpec((1, d), lambda b, i, j: (0, 0)),
                  pl.BlockSpec((1, 1, d), lambda b, i, j: (b, 0, 0)),
                  pl.BlockSpec((1, 1, d), lambda b, i, j: (b, 0, 0)),
                  pl.BlockSpec((d, tn), lambda b, i, j: (0, j))],
        out_specs=out_specs,
        out_shape=out_shape,
        scratch_shapes=[pltpu.VMEM((tm, d), BF16)],
        compiler_params=_params("arbitrary", "arbitrary", "arbitrary"),
        name="norm_mod_matmul",
    )(x, g.reshape(1, d), shift, scale, w)
    return res if emit_h else res[0]


def _row_shift(x, off, pos, width):
    if off == 0:
        return x
    n = x.shape[0]
    r = pltpu.roll(x, (-off) % n, axis=0)
    valid = jnp.logical_and(pos + off >= 0, pos + off < width)
    return jnp.where(valid, r, 0.0)


def _scan_kernel(*refs, reverse, width, final, nblk):
    if final:
        (xr_ref, cw_ref, cb_ref, w_ref, ba_ref, bx_ref, lam_ref, h0_ref, hf_ref, gr_ref,
         o_ref, st_ref, carry) = refs
    else:
        (xr_ref, cw_ref, cb_ref, w_ref, ba_ref, bx_ref, lam_ref, h0_ref,
         o_ref, st_ref, carry) = refs

    @pl.when(pl.program_id(1) == 0)
    def _():
        carry[...] = h0_ref[0]

    x = xr_ref[0]
    tm, rw = x.shape
    bs = rw // nblk
    row = lax.broadcasted_iota(I32, (tm, 1), 0)
    pos = row % width
    cw = cw_ref[...]
    xc = cb_ref[...] + sum(cw[k:k + 1, :] * _row_shift(x, k - 2, pos, width) for k in range(4))

    for n in range(nblk):
        cols = slice(n * bs, (n + 1) * bs)
        xn = xc[:, cols]
        pre = jnp.dot(xn.astype(BF16), w_ref[n], preferred_element_type=F32)
        r = _sigmoid(pre[:, :bs] + ba_ref[:, cols])
        ig = _sigmoid(pre[:, bs:] + bx_ref[:, cols])
        z = -lam_ref[:, cols]
        sp = jnp.maximum(z, 0.0) + jnp.log(1.0 + jnp.exp(-jnp.abs(z)))
        log_a = -LRU_C * r * sp
        a = jnp.exp(log_a)
        u = jnp.sqrt(1.0 - a * a) * (ig * xn)
        d = 1
        while d < tm:
            if reverse:
                a_s = pltpu.roll(a, tm - d, axis=0)
                u_s = pltpu.roll(u, tm - d, axis=0)
                m = row < tm - d
            else:
                a_s = pltpu.roll(a, d, axis=0)
                u_s = pltpu.roll(u, d, axis=0)
                m = row >= d
            u = jnp.where(m, u + a * u_s, u)
            a = jnp.where(m, a * a_s, a)
            d *= 2
        h = u + a * carry[:, cols]
        carry[:, cols] = h[0:1, :] if reverse else h[tm - 1:tm, :]
        if final:
            zb = (hf_ref[0, :, cols] + h) * _gelu(gr_ref[0, :, cols])
            o_ref[0, :, cols] = zb.astype(o_ref.dtype)
        else:
            o_ref[0, :, cols] = h
    st_ref[0] = carry[...]


def _lru_scan(proj, conv_w, conv_b, wcat, ba, bx, lam, h0, *, reverse, width, hf=None):
    bsz, t, _ = proj.shape
    rw = conv_w.shape[1]
    nblk = wcat.shape[0]
    final = hf is not None
    tm = min(256, t)
    nt = t // tm
    tix = (lambda i: nt - 1 - i) if reverse else (lambda i: i)
    vec = lambda a: a.reshape(1, rw)
    in_specs = [pl.BlockSpec((1, tm, rw), lambda b, i: (b, tix(i), 0)),
                _resident((4, rw)), _resident((1, rw)), _resident(wcat.shape),
                _resident((1, rw)), _resident((1, rw)), _resident((1, rw)),
                pl.BlockSpec((1, 1, rw), lambda b, i: (b, 0, 0))]
    args = [proj, conv_w, vec(conv_b), wcat, vec(ba), vec(bx), vec(lam), h0]
    if final:
        in_specs += [pl.BlockSpec((1, tm, rw), lambda b, i: (b, tix(i), 0)),
                     pl.BlockSpec((1, tm, rw), lambda b, i: (b, tix(i), 1))]
        args += [hf, proj]
    out, state = pl.pallas_call(
        functools.partial(_scan_kernel, reverse=reverse, width=width, final=final, nblk=nblk),
        grid=(bsz, nt),
        in_specs=in_specs,
        out_specs=[pl.BlockSpec((1, tm, rw), lambda b, i: (b, tix(i), 0)),
                   pl.BlockSpec((1, 1, rw), lambda b, i: (b, 0, 0))],
        out_shape=[jax.ShapeDtypeStruct((bsz, t, rw), BF16 if final else F32),
                   jax.ShapeDtypeStruct((bsz, 1, rw), F32)],
        scratch_shapes=[pltpu.VMEM((1, rw), F32)],
        compiler_params=_params("arbitrary", "arbitrary"),
        name="lru_scan_bwd" if reverse else "lru_scan_fwd",
    )(*args)
    return out, state


def _merge_kernel(ga_ref, gb_ref, ba_ref, ca_ref, va_ref, zb_ref, cw_ref, cb_ref, wpa_ref, wpb_ref,
                  o_ref, *, width):
    cv = ca_ref[0] * va_ref[0]
    tm = cv.shape[0]
    pos = lax.broadcasted_iota(I32, (tm, 1), 0) % width
    cw = cw_ref[...]
    conv = cb_ref[...] + sum(cw[k:k + 1, :] * _row_shift(cv, k - 1, pos, width) for k in range(3))
    za = (ba_ref[0] * conv).astype(BF16)
    ya = jnp.dot(za, wpa_ref[...], preferred_element_type=F32)
    yb = jnp.dot(zb_ref[0], wpb_ref[...], preferred_element_type=F32)
    o_ref[0] = (_sigmoid(ga_ref[0]) * ya + _sigmoid(gb_ref[0]) * yb).astype(o_ref.dtype)


def _merge(proj, zb, conv_w, conv_b, w_pa, w_pb, *, width):
    bsz, t, _ = proj.shape
    cwid, d = w_pa.shape
    rw = w_pb.shape[0]
    tm = min(256, t)
    ga_blk = (2 * rw) // d
    ca_blk = (2 * rw + 2 * d) // cwid
    pblk = lambda wdt, k: pl.BlockSpec((1, tm, wdt), lambda b, i: (b, i, k))
    return pl.pallas_call(
        functools.partial(_merge_kernel, width=width),
        grid=(bsz, t // tm),
        in_specs=[pblk(d, ga_blk), pblk(d, ga_blk + 1),
                  pblk(cwid, ca_blk), pblk(cwid, ca_blk + 1), pblk(cwid, ca_blk + 2),
                  pl.BlockSpec((1, tm, rw), lambda b, i: (b, i, 0)),
                  _resident((3, cwid)), _resident((1, cwid)),
                  _resident((cwid, d)), _resident((rw, d))],
        out_specs=pl.BlockSpec((1, tm, d), lambda b, i: (b, i, 0)),
        out_shape=jax.ShapeDtypeStruct((bsz, t, d), BF16),
        compiler_params=_params("arbitrary", "arbitrary"),
        name="merge",
    )(proj, proj, proj, proj, proj, zb, conv_w, conv_b.reshape(1, cwid), w_pa, w_pb)


def _proj_res_kernel(m_ref, w_ref, x_ref, g_ref, o_ref):
    y = jnp.dot(m_ref[0], w_ref[...], preferred_element_type=F32)
    o_ref[0] = x_ref[0] + g_ref[0] * y


def _proj_residual(m, w, x, gate):
    bsz, t, d = x.shape
    k = m.shape[2]
    tm = min(512, t)
    return pl.pallas_call(
        _proj_res_kernel,
        grid=(bsz, t // tm),
        in_specs=[pl.BlockSpec((1, tm, k), lambda b, i: (b, i, 0)),
                  _resident((k, d)),
                  pl.BlockSpec((1, tm, d), lambda b, i: (b, i, 0)),
                  pl.BlockSpec((1, 1, d), lambda b, i: (b, 0, 0))],
        out_specs=pl.BlockSpec((1, tm, d), lambda b, i: (b, i, 0)),
        out_shape=jax.ShapeDtypeStruct((bsz, t, d), F32),
        compiler_params=_params("arbitrary", "arbitrary"),
        name="proj_residual",
    )(m, w, x, gate)


def _top16(s, payload=None):
    iota = lax.broadcasted_iota(I32, s.shape, 0).astype(F32)
    vals, picks = [], []
    for _ in range(TOPK):
        m = jnp.max(s, axis=0, keepdims=True)
        idx = jnp.min(jnp.where(s == m, iota, float(s.shape[0])), axis=0, keepdims=True)
        hit = iota == idx
        vals.append(m)
        if payload is None:
            picks.append(idx)
        else:
            picks.append(jnp.sum(jnp.where(hit, payload, 0.0), axis=0, keepdims=True))
        s = jnp.where(hit, -jnp.inf, s)
    return jnp.concatenate(vals, axis=0), jnp.concatenate(picks, axis=0)


def _pair_candidates(v1, i1, v2, i2, keys):
    sub = 8
    row = lax.broadcasted_iota(I32, (sub, 1), 0)
    cand = [v1[0:1] + v2]
    ecand = [i1[0:1] * keys + i2]
    for a in range(1, sub):
        ok = row < TOPK // (a + 1)
        cand.append(jnp.where(ok, v1[a:a + 1] + v2[0:sub], -jnp.inf))
        ecand.append(i1[a:a + 1] * keys + i2[0:sub])
    cand.append(v1[sub:] + v2[0:1])
    ecand.append(i1[sub:] * keys + i2[0:1])
    return jnp.concatenate(cand, axis=0), jnp.concatenate(ecand, axis=0)


def _route_kernel(q_ref, k1_ref, k2_ref, e_ref, w_ref, *, heads, keys):
    q = q_ref[0].astype(BF16)
    half = k1_ref.shape[2]
    dn = (((1,), (1,)), ((), ()))
    es, ws = [], []
    for h in range(heads):
        q1 = q[:, h * 2 * half: h * 2 * half + half]
        q2 = q[:, h * 2 * half + half: (h + 1) * 2 * half]
        s1 = lax.dot_general(k1_ref[h], q1, dn, preferred_element_type=F32)
        s2 = lax.dot_general(k2_ref[h], q2, dn, preferred_element_type=F32)
        v1, i1 = _top16(s1)
        v2, i2 = _top16(s2)
        score, e = _top16(*_pair_candidates(v1, i1, v2, i2, float(keys)))
        p = jnp.exp(score - score[0:1, :])
        ws.append(p / jnp.sum(p, axis=0, keepdims=True))
        es.append(e)
    e_ref[0] = jnp.concatenate(es, axis=0).T.astype(I32)
    w_ref[0] = jnp.concatenate(ws, axis=0).T


def _route(q, k1, k2):
    bsz, t, hq = q.shape
    heads, keys, _ = k1.shape
    tb = min(256, t)
    nsel = heads * TOPK
    return pl.pallas_call(
        functools.partial(_route_kernel, heads=heads, keys=keys),
        grid=(bsz, t // tb),
        in_specs=[pl.BlockSpec((1, tb, hq), lambda b, i: (b, i, 0)),
                  _resident(k1.shape), _resident(k2.shape)],
        out_specs=[pl.BlockSpec((1, tb, nsel), lambda b, i: (b, i, 0)),
                   pl.BlockSpec((1, tb, nsel), lambda b, i: (b, i, 0))],
        out_shape=[jax.ShapeDtypeStruct((bsz, t, nsel), I32),
                   jax.ShapeDtypeStruct((bsz, t, nsel), F32)],
        compiler_params=_params("arbitrary", "arbitrary"),
        name="peer_route",
    )(q, k1, k2)


def _pack_kernel(u_ref, v_ref, o_ref):
    half = u_ref.shape[1] // 2

    def pack(x):
        bits = lax.bitcast_convert_type(x.astype(BF16).astype(F32), U32)
        return (bits[:, :half] >> 16) | (bits[:, half:] & jnp.uint32(0xFFFF0000))

    o_ref[:, :half] = pack(u_ref[...])
    o_ref[:, half:] = pack(v_ref[...])


def _pack_tables(u, v):
    e, d = u.shape
    te = 256
    return pl.pallas_call(
        _pack_kernel,
        grid=(e // te,),
        in_specs=[pl.BlockSpec((te, d), lambda i: (i, 0)), pl.BlockSpec((te, d), lambda i: (i, 0))],
        out_specs=pl.BlockSpec((te, d), lambda i: (i, 0)),
        out_shape=jax.ShapeDtypeStruct((e, d), U32),
        compiler_params=_params("arbitrary"),
        name="pack_tables",
    )(u, v)


def _unpack(words):
    lo = lax.bitcast_convert_type(words << 16, F32)
    hi = lax.bitcast_convert_type(words & jnp.uint32(0xFFFF0000), F32)
    return lo, hi


NSLOT = 4
LOOKAHEAD = NSLOT - 1
SUBLANES = 8


def _expert_kernel(e_ref, w_ref, h_ref, x_ref, g_ref, fg_ref, tab_ref, o_ref, *scratch, final_norm):
    bufs, sems = scratch[:NSLOT], scratch[NSLOT]
    tb, nsel = w_ref.shape[1], w_ref.shape[2]
    d = h_ref.shape[2]
    half = d // 2
    ngrp = nsel // SUBLANES

    def row_copy(t, j, slot):
        return pltpu.make_async_copy(tab_ref.at[pl.ds(e_ref[0, t, j], 1), :],
                                     bufs[slot].at[pl.ds(j, 1), :], sems.at[slot])

    lane = lax.broadcasted_iota(I32, (SUBLANES, nsel), 1)
    sub = lax.broadcasted_iota(I32, (SUBLANES, nsel), 0)

    def token(t, slot, t_next):
        for j in range(nsel):
            row_copy(t, j, slot).wait()
        buf = bufs[slot]
        nslot = (slot + LOOKAHEAD) % NSLOT
        hrow = h_ref[0, pl.ds(t, 1), :]
        xlo = jnp.broadcast_to(hrow[:, :half], (SUBLANES, half))
        xhi = jnp.broadcast_to(hrow[:, half:], (SUBLANES, half))
        wrow = w_ref[0, pl.ds(t, 1), :]
        acc_lo = jnp.zeros((SUBLANES, half), F32)
        acc_hi = jnp.zeros((SUBLANES, half), F32)
        for g in range(ngrp):
            rows = slice(g * SUBLANES, (g + 1) * SUBLANES)
            ulo, uhi = _unpack(buf[rows, :half])
            act = jnp.sum(ulo * xlo + uhi * xhi, axis=1, keepdims=True)
            wcol = jnp.sum(jnp.where(lane == sub + g * SUBLANES, wrow, 0.0), axis=1, keepdims=True)
            coef = wcol * _gelu(act)
            vlo, vhi = _unpack(buf[rows, half:])
            acc_lo = acc_lo + coef * vlo
            acc_hi = acc_hi + coef * vhi
            if t_next is not None:
                for j in range(g * SUBLANES, (g + 1) * SUBLANES):
                    row_copy(t_next, j, nslot).start()
        gate = g_ref[0]
        olo = jnp.sum(acc_lo, axis=0, keepdims=True)
        ohi = jnp.sum(acc_hi, axis=0, keepdims=True)
        o_ref[0, pl.ds(t, 1), :half] = x_ref[0, pl.ds(t, 1), :half] + gate[:, :half] * olo
        o_ref[0, pl.ds(t, 1), half:] = x_ref[0, pl.ds(t, 1), half:] + gate[:, half:] * ohi

    for t in range(LOOKAHEAD):
        for j in range(nsel):
            row_copy(t, j, t).start()

    def body(q, c):
        for s in range(NSLOT):
            t = q * NSLOT + s
            token(t, s, t + LOOKAHEAD)
        return c

    lax.fori_loop(0, tb // NSLOT - 1, body, 0)
    for s in range(NSLOT):
        t = tb - NSLOT + s
        token(t, s, t + LOOKAHEAD if s == 0 else None)

    if final_norm:
        y = o_ref[0]
        ms = jnp.mean(y * y, axis=-1, keepdims=True)
        o_ref[0] = y * lax.rsqrt(ms + EPS) * fg_ref[...]


TOKEN_BLOCK = 256


def _experts(e, w, h, x, gate, final_g, table, *, n_tok, per_batch, final_norm):
    d = x.shape[2]
    nsel = e.shape[2]
    tb = TOKEN_BLOCK
    assert tb % NSLOT == 0 and nsel % SUBLANES == 0 and n_tok % tb == 0 and per_batch % tb == 0
    tok = lambda wdt: pl.BlockSpec((1, tb, wdt), lambda i: (0, i, 0))
    return pl.pallas_call(
        functools.partial(_expert_kernel, final_norm=final_norm),
        grid=(n_tok // tb,),
        in_specs=[pl.BlockSpec((1, tb, nsel), lambda i: (0, i, 0), memory_space=pltpu.SMEM),
                  tok(nsel), tok(d), tok(d),
                  pl.BlockSpec((1, 1, d), lambda i: (i // (per_batch // tb), 0, 0)),
                  pl.BlockSpec((1, d), lambda i: (0, 0)),
                  pl.BlockSpec(memory_space=pl.ANY)],
        out_specs=tok(d),
        out_shape=jax.ShapeDtypeStruct(x.shape, F32),
        scratch_shapes=[pltpu.VMEM((nsel, d), U32)] * NSLOT + [pltpu.SemaphoreType.DMA((NSLOT,))],
        compiler_params=_params("arbitrary"),
        name="peer_experts",
    )(e, w, h, x, gate, final_g.reshape(1, d), table)


SC_LANES = 16
SC_WORKERS = 32
SC_SHARE = 0.39


def _sc_expert_kernel(e_hbm, w_hbm, h_hbm, tab_hbm, o_hbm, idx_v, w_v, h_v, out_v, rows_a, rows_b,
                      sem_a, sem_b, *, start, per_worker):
    lanes = SC_LANES
    nsel = idx_v.shape[0]
    d = h_v.shape[0]
    half = d // 2
    nchunk = nsel // lanes
    wid = lax.axis_index("s") * 2 + lax.axis_index("c")
    lane = lax.iota(I32, lanes)
    bufs = ((rows_a, sem_a), (rows_b, sem_b))

    def gather(c, b):
        rows, sem = bufs[b]
        return pltpu.make_async_copy(tab_hbm.at[idx_v.at[pl.ds(c * lanes, lanes)]], rows, sem)

    def compute(c, rows):
        w16 = w_v[pl.ds(c * lanes, lanes)]

        def ublock(kb, accs):
            off = pl.multiple_of(kb * lanes, lanes)
            hl = h_v[pl.ds(off, lanes)]
            hh = h_v[pl.ds(half + off, lanes)]
            out = []
            for r in range(lanes):
                lo, hi = _unpack(rows[r, pl.ds(off, lanes)])
                out.append(accs[r] + lo * hl + hi * hh)
            return tuple(out)

        zeros = tuple(jnp.zeros((lanes,), F32) for _ in range(lanes))
        accs = plsc.parallel_loop(0, half // lanes, carry=zeros)(ublock)
        act = jnp.zeros((lanes,), F32)
        for r in range(lanes):
            act = jnp.where(lane == r, jnp.sum(accs[r]), act)
        z = 0.7978845608028654 * (act + 0.044715 * act * act * act)
        th = 1.0 - 2.0 / (1.0 + jnp.exp(2.0 * z))
        coef16 = w16 * (0.5 * act * (1.0 + th))
        coefs = [jnp.take(coef16, jnp.full((lanes,), r, I32)) for r in range(lanes)]

        def vblock(kb, carry):
            off = pl.multiple_of(kb * lanes, lanes)
            al = out_v[pl.ds(off, lanes)]
            ah = out_v[pl.ds(half + off, lanes)]
            for r in range(lanes):
                lo, hi = _unpack(rows[r, pl.ds(half + off, lanes)])
                al = al + coefs[r] * lo
                ah = ah + coefs[r] * hi
            out_v[pl.ds(off, lanes)] = al
            out_v[pl.ds(half + off, lanes)] = ah
            return carry

        plsc.parallel_loop(0, half // lanes, carry=jnp.int32(0))(vblock)

    @pl.loop(0, per_worker)
    def _(i):
        t = wid * per_worker + i
        pltpu.sync_copy(e_hbm.at[start + t], idx_v)
        pltpu.sync_copy(w_hbm.at[start + t], w_v)
        pltpu.sync_copy(h_hbm.at[start + t], h_v)
        zero = jnp.zeros((lanes,), F32)
        for kk in range(d // lanes):
            out_v[pl.ds(kk * lanes, lanes)] = zero
        gather(0, 0).start()

        @pl.loop(0, nchunk // 2)
        def _(c2):
            for b in range(2):
                c = c2 * 2 + b
                if b == 0:
                    gather(c + 1, 1).start()
                else:
                    @pl.when(c2 < nchunk // 2 - 1)
                    def _():
                        gather(c + 1, 0).start()
                gather(c, b).wait()
                compute(c, bufs[b][0])

        pltpu.sync_copy(out_v, o_hbm.at[t])


def _experts_sc(e, w, h, table, *, start, count):
    nsel = e.shape[1]
    d = h.shape[1]
    assert count % SC_WORKERS == 0 and nsel % (2 * SC_LANES) == 0
    mesh = plsc.VectorSubcoreMesh(core_axis_name="c", subcore_axis_name="s")
    return pl.kernel(
        functools.partial(_sc_expert_kernel, start=start, per_worker=count // SC_WORKERS),
        mesh=mesh,
        compiler_params=pltpu.CompilerParams(needs_layout_passes=False),
        out_type=jax.ShapeDtypeStruct((count, d), F32),
        scratch_types=[pltpu.VMEM((nsel,), I32), pltpu.VMEM((nsel,), F32),
                       pltpu.VMEM((d,), F32), pltpu.VMEM((d,), F32),
                       pltpu.VMEM((SC_LANES, d), U32), pltpu.VMEM((SC_LANES, d), U32),
                       pltpu.SemaphoreType.DMA, pltpu.SemaphoreType.DMA],
    )(e, w, h, table)


def _residual_kernel(x_ref, p_ref, g_ref, fg_ref, base_ref, o_ref, *, final_norm):
    del base_ref
    y = x_ref[0] + g_ref[0] * p_ref[...]
    if final_norm:
        ms = jnp.mean(y * y, axis=-1, keepdims=True)
        y = y * lax.rsqrt(ms + EPS) * fg_ref[...]
    o_ref[0] = y


def _residual(x, peer, gate, final_g, base, *, start, per_batch, final_norm):
    count, d = peer.shape
    tb = TOKEN_BLOCK
    assert count % tb == 0 and start % tb == 0
    first = start // tb
    return pl.pallas_call(
        functools.partial(_residual_kernel, final_norm=final_norm),
        grid=(count // tb,),
        in_specs=[pl.BlockSpec((1, tb, d), lambda i: (0, first + i, 0)),
                  pl.BlockSpec((tb, d), lambda i: (i, 0)),
                  pl.BlockSpec((1, 1, d), lambda i: ((first + i) // (per_batch // tb), 0, 0)),
                  pl.BlockSpec((1, d), lambda i: (0, 0)),
                  pl.BlockSpec(memory_space=pl.ANY)],
        out_specs=pl.BlockSpec((1, tb, d), lambda i: (0, first + i, 0)),
        out_shape=jax.ShapeDtypeStruct(base.shape, F32),
        input_output_aliases={4: 0},
        compiler_params=_params("arbitrary"),
        name="peer_residual",
    )(x, peer, gate, final_g.reshape(1, d), base)


def _mixer(x, mods, width, h0, prm, *, need_output=True):
    sh1, sc1, g1 = mods
    proj = _norm_mod_matmul(x, prm["norm1_g"], sh1, sc1, prm["w_in"])
    rnn = (prm["conv_b_w"], prm["conv_b_b"])
    hf, st_f = _lru_scan(proj, *rnn, prm["wcat"][0], prm["ba"][0], prm["bx"][0], prm["lam"][0], h0[0],
                         reverse=False, width=width)
    if not need_output:
        _, st_b = _lru_scan(proj, *rnn, prm["wcat"][1], prm["ba"][1], prm["bx"][1], prm["lam"][1], h0[1],
                            reverse=True, width=width)
        return None, (st_f, st_b)
    zb, st_b = _lru_scan(proj, *rnn, prm["wcat"][1], prm["ba"][1], prm["bx"][1], prm["lam"][1], h0[1],
                         reverse=True, width=width, hf=hf)
    merged = _merge(proj, zb, prm["conv_a_w"], prm["conv_a_b"], prm["w_pa"], prm["w_pb"], width=width)
    return _proj_residual(merged, prm["w_o"], x, g1), (st_f, st_b)


def _peer(x, mods, prm, final_g, *, final_norm=False):
    sh2, sc2, g2 = mods
    q, h = _norm_mod_matmul(x, prm["norm2_g"], sh2, sc2, prm["wq"], emit_h=True)
    e, w = _route(q, prm["k1"], prm["k2"])
    bsz, t, d = x.shape
    n = bsz * t
    flat = lambda a: a.reshape(1, n, a.shape[2])
    e, w, h, xf = flat(e), flat(w), flat(h), flat(x)
    assert TOKEN_BLOCK % SC_WORKERS == 0
    n_sc = int(n * SC_SHARE) // TOKEN_BLOCK * TOKEN_BLOCK
    n_tc = n - n_sc
    out = _experts(e, w, h, xf, g2, final_g, prm["table"], n_tok=n_tc, per_batch=t, final_norm=final_norm)
    if n_sc:
        peer_sc = _experts_sc(e[0], w[0], h[0], prm["table"], start=n_tc, count=n_sc)
        out = _residual(xf, peer_sc, g2, final_g, out, start=n_tc, per_batch=t, final_norm=final_norm)
    return out.reshape(bsz, t, d)


def kernel(x, c, ctx, c_ctx, ada_w, ada_b, norm1_g, norm2_g, w_in, conv_a_w, conv_a_b, conv_b_w, conv_b_b,
           lru_wa, lru_ba, lru_wx, lru_bx, lru_lam, w_pa, w_pb, w_o, peer_wq, peer_k1, peer_k2, peer_u,
           peer_v, final_g):
    depth = ada_w.shape[0]
    bsz, _, d = x.shape
    ctx_len = ctx.shape[1]
    cwid = conv_a_w.shape[2]
    rw = conv_b_w.shape[2]
    assert rw == d and (2 * rw + 2 * d) % cwid == 0 and bsz + 1 <= 8

    cond = jnp.zeros((8, d), F32).at[:bsz].set(c).at[bsz].set(c_ctx)
    zero_state = jnp.zeros((2, bsz, 1, rw), F32)
    s3 = 3 * cwid

    for l in range(depth):
        w_in_l = w_in[l]
        prm = dict(
            norm1_g=norm1_g[l], norm2_g=norm2_g[l],
            w_in=jnp.concatenate([w_in_l[:, s3:], w_in_l[:, :s3]], axis=1).astype(BF16),
            conv_a_w=conv_a_w[l], conv_a_b=conv_a_b[l], conv_b_w=conv_b_w[l], conv_b_b=conv_b_b[l],
            wcat=jnp.concatenate([lru_wa[l], lru_wx[l]], axis=-1).astype(BF16),
            ba=lru_ba[l], bx=lru_bx[l], lam=lru_lam[l],
            w_pa=w_pa[l].astype(BF16), w_pb=w_pb[l].astype(BF16), w_o=w_o[l].astype(BF16),
            wq=peer_wq[l].astype(BF16), k1=peer_k1[l].astype(BF16), k2=peer_k2[l].astype(BF16),
            table=_pack_tables(peer_u[l], peer_v[l]),
        )
        m = _adaln(cond, ada_w[l], ada_b[l])
        lat = [m[:bsz, k * d:(k + 1) * d].reshape(bsz, 1, d) for k in range(6)]
        cmod = [jnp.broadcast_to(m[bsz, k * d:(k + 1) * d].reshape(1, 1, d), (bsz, 1, d)) for k in range(6)]
        last = l == depth - 1

        ctx_mix, ctx_state = _mixer(ctx, cmod[0:3], ctx_len, zero_state, prm, need_output=not last)
        if not last:
            ctx = _peer(ctx_mix, cmod[3:6], prm, final_g)
        x = _mixer(x, lat[0:3], GRID_W, ctx_state, prm)[0]
        x = _peer(x, lat[3:6], prm, final_g, final_norm=last)
    return x
```

```python
import functools

import jax
import jax.numpy as jnp
from jax import lax
from jax.experimental import pallas as pl
from jax.experimental.pallas import tpu as pltpu
from jax.experimental.pallas import tpu_sc as plsc

GRID_W = 64
TOPK = 16
LRU_C = 8.0
EPS = 1e-6
VMEM_LIMIT_BYTES = 56 * 1024 * 1024
LANES = 128

F32 = jnp.float32
BF16 = jnp.bfloat16
U32 = jnp.uint32
I32 = jnp.int32


def _params(*sem):
    return pltpu.CompilerParams(dimension_semantics=sem, vmem_limit_bytes=VMEM_LIMIT_BYTES)


def _resident(shape):
    nd = len(shape)
    return pl.BlockSpec(shape, lambda *_: (0,) * nd, pipeline_mode=pl.Buffered(1))


def _tile(n, target):
    if n <= target:
        return n
    t = target - target % LANES
    while n % t:
        t -= LANES
    return t


def _gelu(x):
    return 0.5 * x * (1.0 + jnp.tanh(0.7978845608028654 * (x + 0.044715 * x * x * x)))


def _sigmoid(x):
    return 1.0 / (1.0 + jnp.exp(-x))


def _adaln_kernel(cond_ref, w_ref, b_ref, o_ref):
    cnd = cond_ref[...]
    s = (cnd * _sigmoid(cnd)).astype(BF16)
    o_ref[...] = jnp.dot(s, w_ref[...].astype(BF16), preferred_element_type=F32) + b_ref[...]


def _adaln(cond, w, b):
    rows, d = cond.shape
    n = w.shape[1]
    tn = _tile(n, 1024)
    return pl.pallas_call(
        _adaln_kernel,
        grid=(n // tn,),
        in_specs=[pl.BlockSpec((rows, d), lambda j: (0, 0)),
                  pl.BlockSpec((d, tn), lambda j: (0, j)),
                  pl.BlockSpec((1, tn), lambda j: (0, j))],
        out_specs=pl.BlockSpec((rows, tn), lambda j: (0, j)),
        out_shape=jax.ShapeDtypeStruct((rows, n), F32),
        compiler_params=_params("arbitrary"),
        name="adaln",
    )(cond, w, b.reshape(1, n))


def _nmm_kernel(x_ref, g_ref, sh_ref, sc_ref, w_ref, o_ref, *rest, emit_h):
    if emit_h:
        h_out_ref, h_scr = rest
    else:
        (h_scr,) = rest

    @pl.when(pl.program_id(2) == 0)
    def _():
        x = x_ref[0]
        ms = jnp.mean(x * x, axis=-1, keepdims=True)
        y = x * lax.rsqrt(ms + EPS) * g_ref[...]
        h = y * (1.0 + sc_ref[0]) + sh_ref[0]
        h_scr[...] = h.astype(BF16)
        if emit_h:
            h_out_ref[0] = h

    o_ref[0] = jnp.dot(h_scr[...], w_ref[...], preferred_element_type=F32)


def _norm_mod_matmul(x, g, shift, scale, w, *, emit_h=False):
    bsz, t, d = x.shape
    n = w.shape[1]
    tm = min(512 if emit_h else 1024, t)
    tn = _tile(n, 1024)
    out_shape = [jax.ShapeDtypeStruct((bsz, t, n), F32)]
    out_specs = [pl.BlockSpec((1, tm, tn), lambda b, i, j: (b, i, j))]
    if emit_h:
        out_shape.append(jax.ShapeDtypeStruct((bsz, t, d), F32))
        out_specs.append(pl.BlockSpec((1, tm, d), lambda b, i, j: (b, i, 0)))
    res = pl.pallas_call(
        functools.partial(_nmm_kernel, emit_h=emit_h),
        grid=(bsz, t // tm, n // tn),
        in_specs=[pl.BlockSpec((1, tm, d), lambda b, i, j: (b, i, 0)),
                  pl.BlockSpec((1, d), lambda b, i, j: (0, 0)),
                  pl.BlockSpec((1, 1, d), lambda b, i, j: (b, 0, 0)),
                  pl.BlockSpec((1, 1, d), lambda b, i, j: (b, 0, 0)),
                  pl.BlockSpec((d, tn), lambda b, i, j: (0, j))],
        out_specs=out_specs,
        out_shape=out_shape,
        scratch_shapes=[pltpu.VMEM((tm, d), BF16)],
        compiler_params=_params("arbitrary", "arbitrary", "arbitrary"),
        name="norm_mod_matmul",
    )(x, g.reshape(1, d), shift, scale, w)
    return res if emit_h else res[0]


def _row_shift(x, off, pos, width):
    if off == 0:
        return x
    n = x.shape[0]
    r = pltpu.roll(x, (-off) % n, axis=0)
    valid = jnp.logical_and(pos + off >= 0, pos + off < width)
    return jnp.where(valid, r, 0.0)


def _tile_scan(a, u, state, reverse):
    tm, c = a.shape
    ngrp = tm // SUBLANES
    a3 = a.reshape(ngrp, SUBLANES, c)
    u3 = u.reshape(ngrp, SUBLANES, c)
    sub = lax.broadcasted_iota(I32, (1, SUBLANES, 1), 1)
    for d in (1, 2, 4):
        shift = SUBLANES - d if reverse else d
        keep = sub < SUBLANES - d if reverse else sub >= d
        a_s = pltpu.roll(a3, shift, axis=1)
        u_s = pltpu.roll(u3, shift, axis=1)
        u3 = jnp.where(keep, u3 + a3 * u_s, u3)
        a3 = jnp.where(keep, a3 * a_s, a3)
    hs = [None] * ngrp
    for g in (range(ngrp - 1, -1, -1) if reverse else range(ngrp)):
        hg = u3[g] + a3[g] * state
        hs[g] = hg
        state = hg[0:1, :] if reverse else hg[SUBLANES - 1:SUBLANES, :]
    return jnp.concatenate(hs, axis=0), state


def _scan_kernel(*refs, reverse, width, final, nblk):
    if final:
        (xr_ref, cw_ref, cb_ref, w_ref, ba_ref, bx_ref, lam_ref, h0_ref, hf_ref, gr_ref,
         o_ref, st_ref, carry) = refs
    else:
        (xr_ref, cw_ref, cb_ref, w_ref, ba_ref, bx_ref, lam_ref, h0_ref,
         o_ref, st_ref, carry) = refs

    @pl.when(pl.program_id(1) == 0)
    def _():
        carry[...] = h0_ref[0]

    x = xr_ref[0]
    tm, rw = x.shape
    bs = rw // nblk
    row = lax.broadcasted_iota(I32, (tm, 1), 0)
    pos = row % width
    cw = cw_ref[...]
    xc = cb_ref[...] + sum(cw[k:k + 1, :] * _row_shift(x, k - 2, pos, width) for k in range(4))

    for n in range(nblk):
        cols = slice(n * bs, (n + 1) * bs)
        xn = xc[:, cols]
        pre = jnp.dot(xn.astype(BF16), w_ref[n], preferred_element_type=F32)
        r = _sigmoid(pre[:, :bs] + ba_ref[:, cols])
        ig = _sigmoid(pre[:, bs:] + bx_ref[:, cols])
        z = -lam_ref[:, cols]
        sp = jnp.maximum(z, 0.0) + jnp.log(1.0 + jnp.exp(-jnp.abs(z)))
        log_a = -LRU_C * r * sp
        a = jnp.exp(log_a)
        u = jnp.sqrt(1.0 - a * a) * (ig * xn)
        h, last = _tile_scan(a, u, carry[:, cols], reverse)
        carry[:, cols] = last
        if final:
            zb = (hf_ref[0, :, cols] + h) * _gelu(gr_ref[0, :, cols])
            o_ref[0, :, cols] = zb.astype(o_ref.dtype)
        else:
            o_ref[0, :, cols] = h
    st_ref[0] = carry[...]


def _lru_scan(proj, conv_w, conv_b, wcat, ba, bx, lam, h0, *, reverse, width, hf=None):
    bsz, t, _ = proj.shape
    rw = conv_w.shape[1]
    nblk = wcat.shape[0]
    final = hf is not None
    tm = min(256, t)
    nt = t // tm
    tix = (lambda i: nt - 1 - i) if reverse else (lambda i: i)
    vec = lambda a: a.reshape(1, rw)
    in_specs = [pl.BlockSpec((1, tm, rw), lambda b, i: (b, tix(i), 0)),
                _resident((4, rw)), _resident((1, rw)), _resident(wcat.shape),
                _resident((1, rw)), _resident((1, rw)), _resident((1, rw)),
                pl.BlockSpec((1, 1, rw), lambda b, i: (b, 0, 0))]
    args = [proj, conv_w, vec(conv_b), wcat, vec(ba), vec(bx), vec(lam), h0]
    if final:
        in_specs += [pl.BlockSpec((1, tm, rw), lambda b, i: (b, tix(i), 0)),
                     pl.BlockSpec((1, tm, rw), lambda b, i: (b, tix(i), 1))]
        args += [hf, proj]
    out, state = pl.pallas_call(
        functools.partial(_scan_kernel, reverse=reverse, width=width, final=final, nblk=nblk),
        grid=(bsz, nt),
        in_specs=in_specs,
        out_specs=[pl.BlockSpec((1, tm, rw), lambda b, i: (b, tix(i), 0)),
                   pl.BlockSpec((1, 1, rw), lambda b, i: (b, 0, 0))],
        out_shape=[jax.ShapeDtypeStruct((bsz, t, rw), BF16 if final else F32),
                   jax.ShapeDtypeStruct((bsz, 1, rw), F32)],
        scratch_shapes=[pltpu.VMEM((1, rw), F32)],
        compiler_params=_params("arbitrary", "arbitrary"),
        name="lru_scan_bwd" if reverse else "lru_scan_fwd",
    )(*args)
    return out, state


def _merge_kernel(ga_ref, gb_ref, ba_ref, ca_ref, va_ref, zb_ref, cw_ref, cb_ref, wpa_ref, wpb_ref,
                  o_ref, *, width):
    cv = ca_ref[0] * va_ref[0]
    tm = cv.shape[0]
    pos = lax.broadcasted_iota(I32, (tm, 1), 0) % width
    cw = cw_ref[...]
    conv = cb_ref[...] + sum(cw[k:k + 1, :] * _row_shift(cv, k - 1, pos, width) for k in range(3))
    za = (ba_ref[0] * conv).astype(BF16)
    ya = jnp.dot(za, wpa_ref[...], preferred_element_type=F32)
    yb = jnp.dot(zb_ref[0], wpb_ref[...], preferred_element_type=F32)
    o_ref[0] = (_sigmoid(ga_ref[0]) * ya + _sigmoid(gb_ref[0]) * yb).astype(o_ref.dtype)


def _merge(proj, zb, conv_w, conv_b, w_pa, w_pb, *, width):
    bsz, t, _ = proj.shape
    cwid, d = w_pa.shape
    rw = w_pb.shape[0]
    tm = min(256, t)
    ga_blk = (2 * rw) // d
    ca_blk = (2 * rw + 2 * d) // cwid
    pblk = lambda wdt, k: pl.BlockSpec((1, tm, wdt), lambda b, i: (b, i, k))
    return pl.pallas_call(
        functools.partial(_merge_kernel, width=width),
        grid=(bsz, t // tm),
        in_specs=[pblk(d, ga_blk), pblk(d, ga_blk + 1),
                  pblk(cwid, ca_blk), pblk(cwid, ca_blk + 1), pblk(cwid, ca_blk + 2),
                  pl.BlockSpec((1, tm, rw), lambda b, i: (b, i, 0)),
                  _resident((3, cwid)), _resident((1, cwid)),
                  _resident((cwid, d)), _resident((rw, d))],
        out_specs=pl.BlockSpec((1, tm, d), lambda b, i: (b, i, 0)),
        out_shape=jax.ShapeDtypeStruct((bsz, t, d), BF16),
        compiler_params=_params("arbitrary", "arbitrary"),
        name="merge",
    )(proj, proj, proj, proj, proj, zb, conv_w, conv_b.reshape(1, cwid), w_pa, w_pb)


def _proj_res_kernel(m_ref, w_ref, x_ref, g_ref, o_ref):
    y = jnp.dot(m_ref[0], w_ref[...], preferred_element_type=F32)
    o_ref[0] = x_ref[0] + g_ref[0] * y


def _proj_residual(m, w, x, gate):
    bsz, t, d = x.shape
    k = m.shape[2]
    tm = min(512, t)
    return pl.pallas_call(
        _proj_res_kernel,
        grid=(bsz, t // tm),
        in_specs=[pl.BlockSpec((1, tm, k), lambda b, i: (b, i, 0)),
                  _resident((k, d)),
                  pl.BlockSpec((1, tm, d), lambda b, i: (b, i, 0)),
                  pl.BlockSpec((1, 1, d), lambda b, i: (b, 0, 0))],
        out_specs=pl.BlockSpec((1, tm, d), lambda b, i: (b, i, 0)),
        out_shape=jax.ShapeDtypeStruct((bsz, t, d), F32),
        compiler_params=_params("arbitrary", "arbitrary"),
        name="proj_residual",
    )(m, w, x, gate)


def _top16(s, payload=None):
    iota = lax.broadcasted_iota(I32, s.shape, 0).astype(F32)
    vals, picks = [], []
    for _ in range(TOPK):
        m = jnp.max(s, axis=0, keepdims=True)
        idx = jnp.min(jnp.where(s == m, iota, float(s.shape[0])), axis=0, keepdims=True)
        hit = iota == idx
        vals.append(m)
        if payload is None:
            picks.append(idx)
        else:
            picks.append(jnp.sum(jnp.where(hit, payload, 0.0), axis=0, keepdims=True))
        s = jnp.where(hit, -jnp.inf, s)
    return jnp.concatenate(vals, axis=0), jnp.concatenate(picks, axis=0)


def _pair_candidates(v1, i1, v2, i2, keys):
    sub = 8
    row = lax.broadcasted_iota(I32, (sub, 1), 0)
    cand = [v1[0:1] + v2]
    ecand = [i1[0:1] * keys + i2]
    for a in range(1, sub):
        ok = row < TOPK // (a + 1)
        cand.append(jnp.where(ok, v1[a:a + 1] + v2[0:sub], -jnp.inf))
        ecand.append(i1[a:a + 1] * keys + i2[0:sub])
    cand.append(v1[sub:] + v2[0:1])
    ecand.append(i1[sub:] * keys + i2[0:1])
    return jnp.concatenate(cand, axis=0), jnp.concatenate(ecand, axis=0)


def _route_kernel(q_ref, k1_ref, k2_ref, e_ref, w_ref, *, heads, keys):
    q = q_ref[0].astype(BF16)
    half = k1_ref.shape[2]
    dn = (((1,), (1,)), ((), ()))
    es, ws = [], []
    for h in range(heads):
        q1 = q[:, h * 2 * half: h * 2 * half + half]
        q2 = q[:, h * 2 * half + half: (h + 1) * 2 * half]
        s1 = lax.dot_general(k1_ref[h], q1, dn, preferred_element_type=F32)
        s2 = lax.dot_general(k2_ref[h], q2, dn, preferred_element_type=F32)
        v1, i1 = _top16(s1)
        v2, i2 = _top16(s2)
        score, e = _top16(*_pair_candidates(v1, i1, v2, i2, float(keys)))
        p = jnp.exp(score - score[0:1, :])
        ws.append(p / jnp.sum(p, axis=0, keepdims=True))
        es.append(e)
    e_ref[0] = jnp.concatenate(es, axis=0).T.astype(I32)
    w_ref[0] = jnp.concatenate(ws, axis=0).T


def _route(q, k1, k2):
    bsz, t, hq = q.shape
    heads, keys, _ = k1.shape
    tb = min(256, t)
    nsel = heads * TOPK
    return pl.pallas_call(
        functools.partial(_route_kernel, heads=heads, keys=keys),
        grid=(bsz, t // tb),
        in_specs=[pl.BlockSpec((1, tb, hq), lambda b, i: (b, i, 0)),
                  _resident(k1.shape), _resident(k2.shape)],
        out_specs=[pl.BlockSpec((1, tb, nsel), lambda b, i: (b, i, 0)),
                   pl.BlockSpec((1, tb, nsel), lambda b, i: (b, i, 0))],
        out_shape=[jax.ShapeDtypeStruct((bsz, t, nsel), I32),
                   jax.ShapeDtypeStruct((bsz, t, nsel), F32)],
        compiler_params=_params("arbitrary", "arbitrary"),
        name="peer_route",
    )(q, k1, k2)


def _pack_kernel(u_ref, v_ref, o_ref):
    half = u_ref.shape[1] // 2

    def pack(x):
        bits = lax.bitcast_convert_type(x.astype(BF16).astype(F32), U32)
        return (bits[:, :half] >> 16) | (bits[:, half:] & jnp.uint32(0xFFFF0000))

    o_ref[:, :half] = pack(u_ref[...])
    o_ref[:, half:] = pack(v_ref[...])


def _pack_tables(u, v):
    e, d = u.shape
    te = 256
    return pl.pallas_call(
        _pack_kernel,
        grid=(e // te,),
        in_specs=[pl.BlockSpec((te, d), lambda i: (i, 0)), pl.BlockSpec((te, d), lambda i: (i, 0))],
        out_specs=pl.BlockSpec((te, d), lambda i: (i, 0)),
        out_shape=jax.ShapeDtypeStruct((e, d), U32),
        compiler_params=_params("arbitrary"),
        name="pack_tables",
    )(u, v)


def _unpack(words):
    lo = lax.bitcast_convert_type(words << 16, F32)
    hi = lax.bitcast_convert_type(words & jnp.uint32(0xFFFF0000), F32)
    return lo, hi


NSLOT = 4
LOOKAHEAD = NSLOT - 1
SUBLANES = 8


def _expert_kernel(e_ref, w_ref, h_ref, x_ref, g_ref, fg_ref, tab_ref, o_ref, *scratch, final_norm):
    bufs, sems = scratch[:NSLOT], scratch[NSLOT]
    tb, nsel = w_ref.shape[1], w_ref.shape[2]
    d = h_ref.shape[2]
    half = d // 2
    ngrp = nsel // SUBLANES

    def row_copy(t, j, slot):
        return pltpu.make_async_copy(tab_ref.at[pl.ds(e_ref[0, t, j], 1), :],
                                     bufs[slot].at[pl.ds(j, 1), :], sems.at[slot])

    lane = lax.broadcasted_iota(I32, (SUBLANES, nsel), 1)
    sub = lax.broadcasted_iota(I32, (SUBLANES, nsel), 0)

    def token(t, slot, t_next):
        for j in range(nsel):
            row_copy(t, j, slot).wait()
        buf = bufs[slot]
        nslot = (slot + LOOKAHEAD) % NSLOT
        hrow = h_ref[0, pl.ds(t, 1), :]
        xlo = jnp.broadcast_to(hrow[:, :half], (SUBLANES, half))
        xhi = jnp.broadcast_to(hrow[:, half:], (SUBLANES, half))
        wrow = w_ref[0, pl.ds(t, 1), :]
        acc_lo = jnp.zeros((SUBLANES, half), F32)
        acc_hi = jnp.zeros((SUBLANES, half), F32)
        for g in range(ngrp):
            rows = slice(g * SUBLANES, (g + 1) * SUBLANES)
            ulo, uhi = _unpack(buf[rows, :half])
            act = jnp.sum(ulo * xlo + uhi * xhi, axis=1, keepdims=True)
            wcol = jnp.sum(jnp.where(lane == sub + g * SUBLANES, wrow, 0.0), axis=1, keepdims=True)
            coef = wcol * _gelu(act)
            vlo, vhi = _unpack(buf[rows, half:])
            acc_lo = acc_lo + coef * vlo
            acc_hi = acc_hi + coef * vhi
            if t_next is not None:
                for j in range(g * SUBLANES, (g + 1) * SUBLANES):
                    row_copy(t_next, j, nslot).start()
        gate = g_ref[0]
        olo = jnp.sum(acc_lo, axis=0, keepdims=True)
        ohi = jnp.sum(acc_hi, axis=0, keepdims=True)
        o_ref[0, pl.ds(t, 1), :half] = x_ref[0, pl.ds(t, 1), :half] + gate[:, :half] * olo
        o_ref[0, pl.ds(t, 1), half:] = x_ref[0, pl.ds(t, 1), half:] + gate[:, half:] * ohi

    for t in range(LOOKAHEAD):
        for j in range(nsel):
            row_copy(t, j, t).start()

    def body(q, c):
        for s in range(NSLOT):
            t = q * NSLOT + s
            token(t, s, t + LOOKAHEAD)
        return c

    lax.fori_loop(0, tb // NSLOT - 1, body, 0)
    for s in range(NSLOT):
        t = tb - NSLOT + s
        token(t, s, t + LOOKAHEAD if s == 0 else None)

    if final_norm:
        y = o_ref[0]
        ms = jnp.mean(y * y, axis=-1, keepdims=True)
        o_ref[0] = y * lax.rsqrt(ms + EPS) * fg_ref[...]


TOKEN_BLOCK = 256


def _experts(e, w, h, x, gate, final_g, table, *, n_tok, per_batch, final_norm):
    d = x.shape[2]
    nsel = e.shape[2]
    tb = TOKEN_BLOCK
    assert tb % NSLOT == 0 and nsel % SUBLANES == 0 and n_tok % tb == 0 and per_batch % tb == 0
    tok = lambda wdt: pl.BlockSpec((1, tb, wdt), lambda i: (0, i, 0))
    return pl.pallas_call(
        functools.partial(_expert_kernel, final_norm=final_norm),
        grid=(n_tok // tb,),
        in_specs=[pl.BlockSpec((1, tb, nsel), lambda i: (0, i, 0), memory_space=pltpu.SMEM),
                  tok(nsel), tok(d), tok(d),
                  pl.BlockSpec((1, 1, d), lambda i: (i // (per_batch // tb), 0, 0)),
                  pl.BlockSpec((1, d), lambda i: (0, 0)),
                  pl.BlockSpec(memory_space=pl.ANY)],
        out_specs=tok(d),
        out_shape=jax.ShapeDtypeStruct(x.shape, F32),
        scratch_shapes=[pltpu.VMEM((nsel, d), U32)] * NSLOT + [pltpu.SemaphoreType.DMA((NSLOT,))],
        compiler_params=_params("arbitrary"),
        name="peer_experts",
    )(e, w, h, x, gate, final_g.reshape(1, d), table)


SC_LANES = 16
SC_WORKERS = 32
SC_SHARE = 0.43


def _sc_expert_kernel(e_hbm, w_hbm, h_hbm, tab_hbm, o_hbm, idx_v, w_v, h_v, out_v, rows_a, rows_b,
                      sem_a, sem_b, *, start, per_worker):
    lanes = SC_LANES
    nsel = idx_v.shape[0]
    d = h_v.shape[0]
    half = d // 2
    nchunk = nsel // lanes
    wid = lax.axis_index("s") * 2 + lax.axis_index("c")
    lane = lax.iota(I32, lanes)
    bufs = ((rows_a, sem_a), (rows_b, sem_b))

    def gather(c, b):
        rows, sem = bufs[b]
        return pltpu.make_async_copy(tab_hbm.at[idx_v.at[pl.ds(c * lanes, lanes)]], rows, sem)

    def compute(c, rows):
        w16 = w_v[pl.ds(c * lanes, lanes)]

        def ublock(kb, accs):
            off = pl.multiple_of(kb * lanes, lanes)
            hl = h_v[pl.ds(off, lanes)]
            hh = h_v[pl.ds(half + off, lanes)]
            out = []
            for r in range(lanes):
                lo, hi = _unpack(rows[r, pl.ds(off, lanes)])
                out.append(accs[r] + lo * hl + hi * hh)
            return tuple(out)

        zeros = tuple(jnp.zeros((lanes,), F32) for _ in range(lanes))
        accs = plsc.parallel_loop(0, half // lanes, carry=zeros)(ublock)
        act = jnp.zeros((lanes,), F32)
        for r in range(lanes):
            act = jnp.where(lane == r, jnp.sum(accs[r]), act)
        z = 0.7978845608028654 * (act + 0.044715 * act * act * act)
        th = 1.0 - 2.0 / (1.0 + jnp.exp(2.0 * z))
        coef16 = w16 * (0.5 * act * (1.0 + th))
        coefs = [jnp.take(coef16, jnp.full((lanes,), r, I32)) for r in range(lanes)]

        def vblock(kb, carry):
            off = pl.multiple_of(kb * lanes, lanes)
            al = out_v[pl.ds(off, lanes)]
            ah = out_v[pl.ds(half + off, lanes)]
            for r in range(lanes):
                lo, hi = _unpack(rows[r, pl.ds(half + off, lanes)])
                al = al + coefs[r] * lo
                ah = ah + coefs[r] * hi
            out_v[pl.ds(off, lanes)] = al
            out_v[pl.ds(half + off, lanes)] = ah
            return carry

        plsc.parallel_loop(0, half // lanes, carry=jnp.int32(0))(vblock)

    @pl.loop(0, per_worker)
    def _(i):
        t = wid * per_worker + i
        pltpu.sync_copy(e_hbm.at[start + t], idx_v)
        pltpu.sync_copy(w_hbm.at[start + t], w_v)
        pltpu.sync_copy(h_hbm.at[start + t], h_v)
        zero = jnp.zeros((lanes,), F32)
        for kk in range(d // lanes):
            out_v[pl.ds(kk * lanes, lanes)] = zero
        gather(0, 0).start()

        @pl.loop(0, nchunk // 2)
        def _(c2):
            for b in range(2):
                c = c2 * 2 + b
                if b == 0:
                    gather(c + 1, 1).start()
                else:
                    @pl.when(c2 < nchunk // 2 - 1)
                    def _():
                        gather(c + 1, 0).start()
                gather(c, b).wait()
                compute(c, bufs[b][0])

        pltpu.sync_copy(out_v, o_hbm.at[t])


def _experts_sc(e, w, h, table, *, start, count):
    nsel = e.shape[1]
    d = h.shape[1]
    assert count % SC_WORKERS == 0 and nsel % (2 * SC_LANES) == 0
    mesh = plsc.VectorSubcoreMesh(core_axis_name="c", subcore_axis_name="s")
    return pl.kernel(
        functools.partial(_sc_expert_kernel, start=start, per_worker=count // SC_WORKERS),
        mesh=mesh,
        compiler_params=pltpu.CompilerParams(needs_layout_passes=False),
        out_type=jax.ShapeDtypeStruct((count, d), F32),
        scratch_types=[pltpu.VMEM((nsel,), I32), pltpu.VMEM((nsel,), F32),
                       pltpu.VMEM((d,), F32), pltpu.VMEM((d,), F32),
                       pltpu.VMEM((SC_LANES, d), U32), pltpu.VMEM((SC_LANES, d), U32),
                       pltpu.SemaphoreType.DMA, pltpu.SemaphoreType.DMA],
    )(e, w, h, table)


def _residual_kernel(x_ref, p_ref, g_ref, fg_ref, base_ref, o_ref, *, final_norm):
    del base_ref
    y = x_ref[0] + g_ref[0] * p_ref[...]
    if final_norm:
        ms = jnp.mean(y * y, axis=-1, keepdims=True)
        y = y * lax.rsqrt(ms + EPS) * fg_ref[...]
    o_ref[0] = y


def _residual(x, peer, gate, final_g, base, *, start, per_batch, final_norm):
    count, d = peer.shape
    tb = TOKEN_BLOCK
    assert count % tb == 0 and start % tb == 0
    first = start // tb
    return pl.pallas_call(
        functools.partial(_residual_kernel, final_norm=final_norm),
        grid=(count // tb,),
        in_specs=[pl.BlockSpec((1, tb, d), lambda i: (0, first + i, 0)),
                  pl.BlockSpec((tb, d), lambda i: (i, 0)),
                  pl.BlockSpec((1, 1, d), lambda i: ((first + i) // (per_batch // tb), 0, 0)),
                  pl.BlockSpec((1, d), lambda i: (0, 0)),
                  pl.BlockSpec(memory_space=pl.ANY)],
        out_specs=pl.BlockSpec((1, tb, d), lambda i: (0, first + i, 0)),
        out_shape=jax.ShapeDtypeStruct(base.shape, F32),
        input_output_aliases={4: 0},
        compiler_params=_params("arbitrary"),
        name="peer_residual",
    )(x, peer, gate, final_g.reshape(1, d), base)


def _mixer(x, mods, width, h0, prm, *, need_output=True):
    sh1, sc1, g1 = mods
    proj = _norm_mod_matmul(x, prm["norm1_g"], sh1, sc1, prm["w_in"])
    rnn = (prm["conv_b_w"], prm["conv_b_b"])
    hf, st_f = _lru_scan(proj, *rnn, prm["wcat"][0], prm["ba"][0], prm["bx"][0], prm["lam"][0], h0[0],
                         reverse=False, width=width)
    if not need_output:
        _, st_b = _lru_scan(proj, *rnn, prm["wcat"][1], prm["ba"][1], prm["bx"][1], prm["lam"][1], h0[1],
                            reverse=True, width=width)
        return None, (st_f, st_b)
    zb, st_b = _lru_scan(proj, *rnn, prm["wcat"][1], prm["ba"][1], prm["bx"][1], prm["lam"][1], h0[1],
                         reverse=True, width=width, hf=hf)
    merged = _merge(proj, zb, prm["conv_a_w"], prm["conv_a_b"], prm["w_pa"], prm["w_pb"], width=width)
    return _proj_residual(merged, prm["w_o"], x, g1), (st_f, st_b)


def _peer(x, mods, prm, final_g, *, final_norm=False):
    sh2, sc2, g2 = mods
    q, h = _norm_mod_matmul(x, prm["norm2_g"], sh2, sc2, prm["wq"], emit_h=True)
    e, w = _route(q, prm["k1"], prm["k2"])
    bsz, t, d = x.shape
    n = bsz * t
    flat = lambda a: a.reshape(1, n, a.shape[2])
    e, w, h, xf = flat(e), flat(w), flat(h), flat(x)
    assert TOKEN_BLOCK % SC_WORKERS == 0
    n_sc = int(n * SC_SHARE) // TOKEN_BLOCK * TOKEN_BLOCK
    n_tc = n - n_sc
    out = _experts(e, w, h, xf, g2, final_g, prm["table"], n_tok=n_tc, per_batch=t, final_norm=final_norm)
    if n_sc:
        peer_sc = _experts_sc(e[0], w[0], h[0], prm["table"], start=n_tc, count=n_sc)
        out = _residual(xf, peer_sc, g2, final_g, out, start=n_tc, per_batch=t, final_norm=final_norm)
    return out.reshape(bsz, t, d)


def kernel(x, c, ctx, c_ctx, ada_w, ada_b, norm1_g, norm2_g, w_in, conv_a_w, conv_a_b, conv_b_w, conv_b_b,
           lru_wa, lru_ba, lru_wx, lru_bx, lru_lam, w_pa, w_pb, w_o, peer_wq, peer_k1, peer_k2, peer_u,
           peer_v, final_g):
    depth = ada_w.shape[0]
    bsz, _, d = x.shape
    ctx_len = ctx.shape[1]
    cwid = conv_a_w.shape[2]
    rw = conv_b_w.shape[2]
    assert rw == d and (2 * rw + 2 * d) % cwid == 0 and bsz + 1 <= 8

    cond = jnp.zeros((8, d), F32).at[:bsz].set(c).at[bsz].set(c_ctx)
    zero_state = jnp.zeros((2, bsz, 1, rw), F32)
    s3 = 3 * cwid

    for l in range(depth):
        w_in_l = w_in[l]
        prm = dict(
            norm1_g=norm1_g[l], norm2_g=norm2_g[l],
            w_in=jnp.concatenate([w_in_l[:, s3:], w_in_l[:, :s3]], axis=1).astype(BF16),
            conv_a_w=conv_a_w[l], conv_a_b=conv_a_b[l], conv_b_w=conv_b_w[l], conv_b_b=conv_b_b[l],
            wcat=jnp.concatenate([lru_wa[l], lru_wx[l]], axis=-1).astype(BF16),
            ba=lru_ba[l], bx=lru_bx[l], lam=lru_lam[l],
            w_pa=w_pa[l].astype(BF16), w_pb=w_pb[l].astype(BF16), w_o=w_o[l].astype(BF16),
            wq=peer_wq[l].astype(BF16), k1=peer_k1[l].astype(BF16), k2=peer_k2[l].astype(BF16),
            table=_pack_tables(peer_u[l], peer_v[l]),
        )
        m = _adaln(cond, ada_w[l], ada_b[l])
        lat = [m[:bsz, k * d:(k + 1) * d].reshape(bsz, 1, d) for k in range(6)]
        cmod = [jnp.broadcast_to(m[bsz, k * d:(k + 1) * d].reshape(1, 1, d), (bsz, 1, d)) for k in range(6)]
        last = l == depth - 1

        ctx_mix, ctx_state = _mixer(ctx, cmod[0:3], ctx_len, zero_state, prm, need_output=not last)
        if not last:
            ctx = _peer(ctx_mix, cmod[3:6], prm, final_g)
        x = _mixer(x, lat[0:3], GRID_W, ctx_state, prm)[0]
        x = _peer(x, lat[3:6], prm, final_g, final_norm=last)
    return x
```

```python
import functools

import jax
import jax.numpy as jnp
from jax import lax
from jax.experimental import pallas as pl
from jax.experimental.pallas import tpu as pltpu
from jax.experimental.pallas import tpu_sc as plsc

GRID_W = 64
TOPK = 16
LRU_C = 8.0
EPS = 1e-6
VMEM_LIMIT_BYTES = 56 * 1024 * 1024
LANES = 128

F32 = jnp.float32
BF16 = jnp.bfloat16
U32 = jnp.uint32
I32 = jnp.int32


def _params(*sem):
    return pltpu.CompilerParams(dimension_semantics=sem, vmem_limit_bytes=VMEM_LIMIT_BYTES)


def _resident(shape):
    nd = len(shape)
    return pl.BlockSpec(shape, lambda *_: (0,) * nd, pipeline_mode=pl.Buffered(1))


def _tile(n, target):
    if n <= target:
        return n
    t = target - target % LANES
    while n % t:
        t -= LANES
    return t


def _gelu(x):
    return 0.5 * x * (1.0 + jnp.tanh(0.7978845608028654 * (x + 0.044715 * x * x * x)))


def _sigmoid(x):
    return 1.0 / (1.0 + jnp.exp(-x))


def _adaln_kernel(cond_ref, w_ref, b_ref, o_ref):
    cnd = cond_ref[...]
    s = (cnd * _sigmoid(cnd)).astype(BF16)
    o_ref[...] = jnp.dot(s, w_ref[...].astype(BF16), preferred_element_type=F32) + b_ref[...]


def _adaln(cond, w, b):
    rows, d = cond.shape
    n = w.shape[1]
    tn = _tile(n, 1024)
    return pl.pallas_call(
        _adaln_kernel,
        grid=(n // tn,),
        in_specs=[pl.BlockSpec((rows, d), lambda j: (0, 0)),
                  pl.BlockSpec((d, tn), lambda j: (0, j)),
                  pl.BlockSpec((1, tn), lambda j: (0, j))],
        out_specs=pl.BlockSpec((rows, tn), lambda j: (0, j)),
        out_shape=jax.ShapeDtypeStruct((rows, n), F32),
        compiler_params=_params("arbitrary"),
        name="adaln",
    )(cond, w, b.reshape(1, n))


def _nmm_kernel(x_ref, g_ref, sh_ref, sc_ref, w_ref, o_ref, *rest, emit_h):
    if emit_h:
        h_out_ref, h_scr = rest
    else:
        (h_scr,) = rest

    @pl.when(pl.program_id(2) == 0)
    def _():
        x = x_ref[0]
        ms = jnp.mean(x * x, axis=-1, keepdims=True)
        y = x * lax.rsqrt(ms + EPS) * g_ref[...]
        h = y * (1.0 + sc_ref[0]) + sh_ref[0]
        h_scr[...] = h.astype(BF16)
        if emit_h:
            h_out_ref[0] = h

    o_ref[0] = jnp.dot(h_scr[...], w_ref[...], preferred_element_type=F32)


def _norm_mod_matmul(x, g, shift, scale, w, *, emit_h=False):
    bsz, t, d = x.shape
    n = w.shape[1]
    tm = min(512 if emit_h else 1024, t)
    tn = _tile(n, 1024)
    out_shape = [jax.ShapeDtypeStruct((bsz, t, n), F32)]
    out_specs = [pl.BlockSpec((1, tm, tn), lambda b, i, j: (b, i, j))]
    if emit_h:
        out_shape.append(jax.ShapeDtypeStruct((bsz, t, d), F32))
        out_specs.append(pl.BlockSpec((1, tm, d), lambda b, i, j: (b, i, 0)))
    res = pl.pallas_call(
        functools.partial(_nmm_kernel, emit_h=emit_h),
        grid=(bsz, t // tm, n // tn),
        in_specs=[pl.BlockSpec((1, tm, d), lambda b, i, j: (b, i, 0)),
                  pl.BlockSpec((1, d), lambda b, i, j: (0, 0)),
                  pl.BlockSpec((1, 1, d), lambda b, i, j: (b, 0, 0)),
                  pl.BlockSpec((1, 1, d), lambda b, i, j: (b, 0, 0)),
                  pl.BlockSpec((d, tn), lambda b, i, j: (0, j))],
        out_specs=out_specs,
        out_shape=out_shape,
        scratch_shapes=[pltpu.VMEM((tm, d), BF16)],
        compiler_params=_params("arbitrary", "arbitrary", "arbitrary"),
        name="norm_mod_matmul",
    )(x, g.reshape(1, d), shift, scale, w)
    return res if emit_h else res[0]


ROW_TILE = 512


def _norm_mod_matmul_rows(xf, g, shift, scale, w, *, start, count, per_batch):
    d = xf.shape[2]
    n = w.shape[1]
    tm = min(ROW_TILE, per_batch)
    tn = _tile(n, 1024)
    assert start % tm == 0 and count % tm == 0 and per_batch % tm == 0
    first, per = start // tm, per_batch // tm
    return pl.pallas_call(
        functools.partial(_nmm_kernel, emit_h=True),
        grid=(1, count // tm, n // tn),
        in_specs=[pl.BlockSpec((1, tm, d), lambda b, i, j: (0, first + i, 0)),
                  pl.BlockSpec((1, d), lambda b, i, j: (0, 0)),
                  pl.BlockSpec((1, 1, d), lambda b, i, j: ((first + i) // per, 0, 0)),
                  pl.BlockSpec((1, 1, d), lambda b, i, j: ((first + i) // per, 0, 0)),
                  pl.BlockSpec((d, tn), lambda b, i, j: (0, j))],
        out_specs=[pl.BlockSpec((1, tm, tn), lambda b, i, j: (0, i, j)),
                   pl.BlockSpec((1, tm, d), lambda b, i, j: (0, i, 0))],
        out_shape=[jax.ShapeDtypeStruct((1, count, n), F32), jax.ShapeDtypeStruct((1, count, d), F32)],
        scratch_shapes=[pltpu.VMEM((tm, d), BF16)],
        compiler_params=_params("arbitrary", "arbitrary", "arbitrary"),
        name="norm_mod_matmul",
    )(xf, g.reshape(1, d), shift, scale, w)


def _row_shift(x, off, pos, width):
    if off == 0:
        return x
    n = x.shape[0]
    r = pltpu.roll(x, (-off) % n, axis=0)
    valid = jnp.logical_and(pos + off >= 0, pos + off < width)
    return jnp.where(valid, r, 0.0)


def _tile_scan(a, u, state, reverse):
    tm, c = a.shape
    ngrp = tm // SUBLANES
    a3 = a.reshape(ngrp, SUBLANES, c)
    u3 = u.reshape(ngrp, SUBLANES, c)
    sub = lax.broadcasted_iota(I32, (1, SUBLANES, 1), 1)
    for d in (1, 2, 4):
        shift = SUBLANES - d if reverse else d
        keep = sub < SUBLANES - d if reverse else sub >= d
        a_s = pltpu.roll(a3, shift, axis=1)
        u_s = pltpu.roll(u3, shift, axis=1)
        u3 = jnp.where(keep, u3 + a3 * u_s, u3)
        a3 = jnp.where(keep, a3 * a_s, a3)
    hs = [None] * ngrp
    for g in (range(ngrp - 1, -1, -1) if reverse else range(ngrp)):
        hg = u3[g] + a3[g] * state
        hs[g] = hg
        state = hg[0:1, :] if reverse else hg[SUBLANES - 1:SUBLANES, :]
    return jnp.concatenate(hs, axis=0), state


def _scan_kernel(*refs, reverse, width, final, nblk):
    if final:
        (xr_ref, cw_ref, cb_ref, w_ref, ba_ref, bx_ref, lam_ref, h0_ref, hf_ref, gr_ref,
         o_ref, st_ref, carry) = refs
    else:
        (xr_ref, cw_ref, cb_ref, w_ref, ba_ref, bx_ref, lam_ref, h0_ref,
         o_ref, st_ref, carry) = refs

    @pl.when(pl.program_id(1) == 0)
    def _():
        carry[...] = h0_ref[0]

    x = xr_ref[0]
    tm, rw = x.shape
    bs = rw // nblk
    row = lax.broadcasted_iota(I32, (tm, 1), 0)
    pos = row % width
    cw = cw_ref[...]
    xc = cb_ref[...] + sum(cw[k:k + 1, :] * _row_shift(x, k - 2, pos, width) for k in range(4))

    for n in range(nblk):
        cols = slice(n * bs, (n + 1) * bs)
        xn = xc[:, cols]
        pre = jnp.dot(xn.astype(BF16), w_ref[n], preferred_element_type=F32)
        r = _sigmoid(pre[:, :bs] + ba_ref[:, cols])
        ig = _sigmoid(pre[:, bs:] + bx_ref[:, cols])
        z = -lam_ref[:, cols]
        sp = jnp.maximum(z, 0.0) + jnp.log(1.0 + jnp.exp(-jnp.abs(z)))
        log_a = -LRU_C * r * sp
        a = jnp.exp(log_a)
        u = jnp.sqrt(1.0 - a * a) * (ig * xn)
        h, last = _tile_scan(a, u, carry[:, cols], reverse)
        carry[:, cols] = last
        if final:
            zb = (hf_ref[0, :, cols] + h) * _gelu(gr_ref[0, :, cols])
            o_ref[0, :, cols] = zb.astype(o_ref.dtype)
        else:
            o_ref[0, :, cols] = h
    st_ref[0] = carry[...]


def _lru_scan(proj, conv_w, conv_b, wcat, ba, bx, lam, h0, *, reverse, width, hf=None):
    bsz, t, _ = proj.shape
    rw = conv_w.shape[1]
    nblk = wcat.shape[0]
    final = hf is not None
    tm = min(256, t)
    nt = t // tm
    tix = (lambda i: nt - 1 - i) if reverse else (lambda i: i)
    vec = lambda a: a.reshape(1, rw)
    in_specs = [pl.BlockSpec((1, tm, rw), lambda b, i: (b, tix(i), 0)),
                _resident((4, rw)), _resident((1, rw)), _resident(wcat.shape),
                _resident((1, rw)), _resident((1, rw)), _resident((1, rw)),
                pl.BlockSpec((1, 1, rw), lambda b, i: (b, 0, 0))]
    args = [proj, conv_w, vec(conv_b), wcat, vec(ba), vec(bx), vec(lam), h0]
    if final:
        in_specs += [pl.BlockSpec((1, tm, rw), lambda b, i: (b, tix(i), 0)),
                     pl.BlockSpec((1, tm, rw), lambda b, i: (b, tix(i), 1))]
        args += [hf, proj]
    out, state = pl.pallas_call(
        functools.partial(_scan_kernel, reverse=reverse, width=width, final=final, nblk=nblk),
        grid=(bsz, nt),
        in_specs=in_specs,
        out_specs=[pl.BlockSpec((1, tm, rw), lambda b, i: (b, tix(i), 0)),
                   pl.BlockSpec((1, 1, rw), lambda b, i: (b, 0, 0))],
        out_shape=[jax.ShapeDtypeStruct((bsz, t, rw), BF16 if final else F32),
                   jax.ShapeDtypeStruct((bsz, 1, rw), F32)],
        scratch_shapes=[pltpu.VMEM((1, rw), F32)],
        compiler_params=_params("arbitrary", "arbitrary"),
        name="lru_scan_bwd" if reverse else "lru_scan_fwd",
    )(*args)
    return out, state


def _merge_kernel(ga_ref, gb_ref, ba_ref, ca_ref, va_ref, zb_ref, cw_ref, cb_ref, wpa_ref, wpb_ref,
                  o_ref, *, width):
    cv = ca_ref[0] * va_ref[0]
    tm = cv.shape[0]
    pos = lax.broadcasted_iota(I32, (tm, 1), 0) % width
    cw = cw_ref[...]
    conv = cb_ref[...] + sum(cw[k:k + 1, :] * _row_shift(cv, k - 1, pos, width) for k in range(3))
    za = (ba_ref[0] * conv).astype(BF16)
    ya = jnp.dot(za, wpa_ref[...], preferred_element_type=F32)
    yb = jnp.dot(zb_ref[0], wpb_ref[...], preferred_element_type=F32)
    o_ref[0] = (_sigmoid(ga_ref[0]) * ya + _sigmoid(gb_ref[0]) * yb).astype(o_ref.dtype)


def _merge(proj, zb, conv_w, conv_b, w_pa, w_pb, *, width):
    bsz, t, _ = proj.shape
    cwid, d = w_pa.shape
    rw = w_pb.shape[0]
    tm = min(256, t)
    ga_blk = (2 * rw) // d
    ca_blk = (2 * rw + 2 * d) // cwid
    pblk = lambda wdt, k: pl.BlockSpec((1, tm, wdt), lambda b, i: (b, i, k))
    return pl.pallas_call(
        functools.partial(_merge_kernel, width=width),
        grid=(bsz, t // tm),
        in_specs=[pblk(d, ga_blk), pblk(d, ga_blk + 1),
                  pblk(cwid, ca_blk), pblk(cwid, ca_blk + 1), pblk(cwid, ca_blk + 2),
                  pl.BlockSpec((1, tm, rw), lambda b, i: (b, i, 0)),
                  _resident((3, cwid)), _resident((1, cwid)),
                  _resident((cwid, d)), _resident((rw, d))],
        out_specs=pl.BlockSpec((1, tm, d), lambda b, i: (b, i, 0)),
        out_shape=jax.ShapeDtypeStruct((bsz, t, d), BF16),
        compiler_params=_params("arbitrary", "arbitrary"),
        name="merge",
    )(proj, proj, proj, proj, proj, zb, conv_w, conv_b.reshape(1, cwid), w_pa, w_pb)


def _proj_res_kernel(m_ref, w_ref, x_ref, g_ref, o_ref):
    y = jnp.dot(m_ref[0], w_ref[...], preferred_element_type=F32)
    o_ref[0] = x_ref[0] + g_ref[0] * y


def _proj_residual(m, w, x, gate):
    bsz, t, d = x.shape
    k = m.shape[2]
    tm = min(512, t)
    return pl.pallas_call(
        _proj_res_kernel,
        grid=(bsz, t // tm),
        in_specs=[pl.BlockSpec((1, tm, k), lambda b, i: (b, i, 0)),
                  _resident((k, d)),
                  pl.BlockSpec((1, tm, d), lambda b, i: (b, i, 0)),
                  pl.BlockSpec((1, 1, d), lambda b, i: (b, 0, 0))],
        out_specs=pl.BlockSpec((1, tm, d), lambda b, i: (b, i, 0)),
        out_shape=jax.ShapeDtypeStruct((bsz, t, d), F32),
        compiler_params=_params("arbitrary", "arbitrary"),
        name="proj_residual",
    )(m, w, x, gate)


def _top16(s, payload=None):
    iota = lax.broadcasted_iota(I32, s.shape, 0).astype(F32)
    vals, picks = [], []
    for _ in range(TOPK):
        m = jnp.max(s, axis=0, keepdims=True)
        idx = jnp.min(jnp.where(s == m, iota, float(s.shape[0])), axis=0, keepdims=True)
        hit = iota == idx
        vals.append(m)
        if payload is None:
            picks.append(idx)
        else:
            picks.append(jnp.sum(jnp.where(hit, payload, 0.0), axis=0, keepdims=True))
        s = jnp.where(hit, -jnp.inf, s)
    return jnp.concatenate(vals, axis=0), jnp.concatenate(picks, axis=0)


def _pair_candidates(v1, i1, v2, i2, keys):
    sub = 8
    row = lax.broadcasted_iota(I32, (sub, 1), 0)
    cand = [v1[0:1] + v2]
    ecand = [i1[0:1] * keys + i2]
    for a in range(1, sub):
        ok = row < TOPK // (a + 1)
        cand.append(jnp.where(ok, v1[a:a + 1] + v2[0:sub], -jnp.inf))
        ecand.append(i1[a:a + 1] * keys + i2[0:sub])
    cand.append(v1[sub:] + v2[0:1])
    ecand.append(i1[sub:] * keys + i2[0:1])
    return jnp.concatenate(cand, axis=0), jnp.concatenate(ecand, axis=0)


def _route_kernel(q_ref, k1_ref, k2_ref, e_ref, w_ref, *, heads, keys):
    q = q_ref[0].astype(BF16)
    half = k1_ref.shape[2]
    dn = (((1,), (1,)), ((), ()))
    es, ws = [], []
    for h in range(heads):
        q1 = q[:, h * 2 * half: h * 2 * half + half]
        q2 = q[:, h * 2 * half + half: (h + 1) * 2 * half]
        s1 = lax.dot_general(k1_ref[h], q1, dn, preferred_element_type=F32)
        s2 = lax.dot_general(k2_ref[h], q2, dn, preferred_element_type=F32)
        v1, i1 = _top16(s1)
        v2, i2 = _top16(s2)
        score, e = _top16(*_pair_candidates(v1, i1, v2, i2, float(keys)))
        p = jnp.exp(score - score[0:1, :])
        ws.append(p / jnp.sum(p, axis=0, keepdims=True))
        es.append(e)
    e_ref[0] = jnp.concatenate(es, axis=0).T.astype(I32)
    w_ref[0] = jnp.concatenate(ws, axis=0).T


def _route(q, k1, k2):
    bsz, t, hq = q.shape
    heads, keys, _ = k1.shape
    tb = min(256, t)
    nsel = heads * TOPK
    return pl.pallas_call(
        functools.partial(_route_kernel, heads=heads, keys=keys),
        grid=(bsz, t // tb),
        in_specs=[pl.BlockSpec((1, tb, hq), lambda b, i: (b, i, 0)),
                  _resident(k1.shape), _resident(k2.shape)],
        out_specs=[pl.BlockSpec((1, tb, nsel), lambda b, i: (b, i, 0)),
                   pl.BlockSpec((1, tb, nsel), lambda b, i: (b, i, 0))],
        out_shape=[jax.ShapeDtypeStruct((bsz, t, nsel), I32),
                   jax.ShapeDtypeStruct((bsz, t, nsel), F32)],
        compiler_params=_params("arbitrary", "arbitrary"),
        name="peer_route",
    )(q, k1, k2)


def _pack_kernel(u_ref, v_ref, o_ref):
    half = u_ref.shape[1] // 2

    def pack(x):
        bits = lax.bitcast_convert_type(x.astype(BF16).astype(F32), U32)
        return (bits[:, :half] >> 16) | (bits[:, half:] & jnp.uint32(0xFFFF0000))

    o_ref[:, :half] = pack(u_ref[...])
    o_ref[:, half:] = pack(v_ref[...])


def _pack_tables(u, v):
    e, d = u.shape
    te = 256
    return pl.pallas_call(
        _pack_kernel,
        grid=(e // te,),
        in_specs=[pl.BlockSpec((te, d), lambda i: (i, 0)), pl.BlockSpec((te, d), lambda i: (i, 0))],
        out_specs=pl.BlockSpec((te, d), lambda i: (i, 0)),
        out_shape=jax.ShapeDtypeStruct((e, d), U32),
        compiler_params=_params("arbitrary"),
        name="pack_tables",
    )(u, v)


def _unpack(words):
    lo = lax.bitcast_convert_type(words << 16, F32)
    hi = lax.bitcast_convert_type(words & jnp.uint32(0xFFFF0000), F32)
    return lo, hi


NSLOT = 4
LOOKAHEAD = NSLOT - 1
SUBLANES = 8


def _expert_kernel(e_ref, w_ref, h_ref, x_ref, g_ref, fg_ref, tab_ref, o_ref, *scratch, final_norm):
    bufs, sems = scratch[:NSLOT], scratch[NSLOT]
    tb, nsel = w_ref.shape[1], w_ref.shape[2]
    d = h_ref.shape[2]
    half = d // 2
    ngrp = nsel // SUBLANES

    def row_copy(t, j, slot):
        return pltpu.make_async_copy(tab_ref.at[pl.ds(e_ref[0, t, j], 1), :],
                                     bufs[slot].at[pl.ds(j, 1), :], sems.at[slot])

    lane = lax.broadcasted_iota(I32, (SUBLANES, nsel), 1)
    sub = lax.broadcasted_iota(I32, (SUBLANES, nsel), 0)

    def token(t, slot, t_next):
        for j in range(nsel):
            row_copy(t, j, slot).wait()
        buf = bufs[slot]
        nslot = (slot + LOOKAHEAD) % NSLOT
        hrow = h_ref[0, pl.ds(t, 1), :]
        xlo = jnp.broadcast_to(hrow[:, :half], (SUBLANES, half))
        xhi = jnp.broadcast_to(hrow[:, half:], (SUBLANES, half))
        wrow = w_ref[0, pl.ds(t, 1), :]
        acc_lo = jnp.zeros((SUBLANES, half), F32)
        acc_hi = jnp.zeros((SUBLANES, half), F32)
        for g in range(ngrp):
            rows = slice(g * SUBLANES, (g + 1) * SUBLANES)
            ulo, uhi = _unpack(buf[rows, :half])
            act = jnp.sum(ulo * xlo + uhi * xhi, axis=1, keepdims=True)
            wcol = jnp.sum(jnp.where(lane == sub + g * SUBLANES, wrow, 0.0), axis=1, keepdims=True)
            coef = wcol * _gelu(act)
            vlo, vhi = _unpack(buf[rows, half:])
            acc_lo = acc_lo + coef * vlo
            acc_hi = acc_hi + coef * vhi
            if t_next is not None:
                for j in range(g * SUBLANES, (g + 1) * SUBLANES):
                    row_copy(t_next, j, nslot).start()
        gate = g_ref[0]
        olo = jnp.sum(acc_lo, axis=0, keepdims=True)
        ohi = jnp.sum(acc_hi, axis=0, keepdims=True)
        o_ref[0, pl.ds(t, 1), :half] = x_ref[0, pl.ds(t, 1), :half] + gate[:, :half] * olo
        o_ref[0, pl.ds(t, 1), half:] = x_ref[0, pl.ds(t, 1), half:] + gate[:, half:] * ohi

    for t in range(LOOKAHEAD):
        for j in range(nsel):
            row_copy(t, j, t).start()

    def body(q, c):
        for s in range(NSLOT):
            t = q * NSLOT + s
            token(t, s, t + LOOKAHEAD)
        return c

    lax.fori_loop(0, tb // NSLOT - 1, body, 0)
    for s in range(NSLOT):
        t = tb - NSLOT + s
        token(t, s, t + LOOKAHEAD if s == 0 else None)

    if final_norm:
        y = o_ref[0]
        ms = jnp.mean(y * y, axis=-1, keepdims=True)
        o_ref[0] = y * lax.rsqrt(ms + EPS) * fg_ref[...]


TOKEN_BLOCK = 256


def _experts(e, w, h, x, gate, final_g, table, *, n_tok, per_batch, final_norm):
    d = x.shape[2]
    nsel = e.shape[2]
    tb = TOKEN_BLOCK
    assert tb % NSLOT == 0 and nsel % SUBLANES == 0 and n_tok % tb == 0 and per_batch % tb == 0
    tok = lambda wdt: pl.BlockSpec((1, tb, wdt), lambda i: (0, i, 0))
    return pl.pallas_call(
        functools.partial(_expert_kernel, final_norm=final_norm),
        grid=(n_tok // tb,),
        in_specs=[pl.BlockSpec((1, tb, nsel), lambda i: (0, i, 0), memory_space=pltpu.SMEM),
                  tok(nsel), tok(d), tok(d),
                  pl.BlockSpec((1, 1, d), lambda i: (i // (per_batch // tb), 0, 0)),
                  pl.BlockSpec((1, d), lambda i: (0, 0)),
                  pl.BlockSpec(memory_space=pl.ANY)],
        out_specs=tok(d),
        out_shape=jax.ShapeDtypeStruct(x.shape, F32),
        scratch_shapes=[pltpu.VMEM((nsel, d), U32)] * NSLOT + [pltpu.SemaphoreType.DMA((NSLOT,))],
        compiler_params=_params("arbitrary"),
        name="peer_experts",
    )(e, w, h, x, gate, final_g.reshape(1, d), table)


SC_LANES = 16
SC_WORKERS = 32
SC_SHARE = 0.47


def _sc_expert_kernel(e_hbm, w_hbm, h_hbm, tab_hbm, o_hbm, idx_v, w_v, h_v, out_v, rows_a, rows_b,
                      sem_a, sem_b, *, start, per_worker):
    lanes = SC_LANES
    nsel = idx_v.shape[0]
    d = h_v.shape[0]
    half = d // 2
    nchunk = nsel // lanes
    wid = lax.axis_index("s") * 2 + lax.axis_index("c")
    lane = lax.iota(I32, lanes)
    bufs = ((rows_a, sem_a), (rows_b, sem_b))

    def gather(c, b):
        rows, sem = bufs[b]
        return pltpu.make_async_copy(tab_hbm.at[idx_v.at[pl.ds(c * lanes, lanes)]], rows, sem)

    def compute(c, rows):
        w16 = w_v[pl.ds(c * lanes, lanes)]

        def ublock(kb, accs):
            off = pl.multiple_of(kb * lanes, lanes)
            hl = h_v[pl.ds(off, lanes)]
            hh = h_v[pl.ds(half + off, lanes)]
            out = []
            for r in range(lanes):
                lo, hi = _unpack(rows[r, pl.ds(off, lanes)])
                out.append(accs[r] + lo * hl + hi * hh)
            return tuple(out)

        zeros = tuple(jnp.zeros((lanes,), F32) for _ in range(lanes))
        accs = plsc.parallel_loop(0, half // lanes, carry=zeros)(ublock)
        act = jnp.zeros((lanes,), F32)
        for r in range(lanes):
            act = jnp.where(lane == r, jnp.sum(accs[r]), act)
        z = 0.7978845608028654 * (act + 0.044715 * act * act * act)
        th = 1.0 - 2.0 / (1.0 + jnp.exp(2.0 * z))
        coef16 = w16 * (0.5 * act * (1.0 + th))
        coefs = [jnp.take(coef16, jnp.full((lanes,), r, I32)) for r in range(lanes)]

        def vblock(kb, carry):
            off = pl.multiple_of(kb * lanes, lanes)
            al = out_v[pl.ds(off, lanes)]
            ah = out_v[pl.ds(half + off, lanes)]
            for r in range(lanes):
                lo, hi = _unpack(rows[r, pl.ds(half + off, lanes)])
                al = al + coefs[r] * lo
                ah = ah + coefs[r] * hi
            out_v[pl.ds(off, lanes)] = al
            out_v[pl.ds(half + off, lanes)] = ah
            return carry

        plsc.parallel_loop(0, half // lanes, carry=jnp.int32(0))(vblock)

    @pl.loop(0, per_worker)
    def _(i):
        t = wid * per_worker + i
        pltpu.sync_copy(e_hbm.at[start + t], idx_v)
        pltpu.sync_copy(w_hbm.at[start + t], w_v)
        pltpu.sync_copy(h_hbm.at[start + t], h_v)
        zero = jnp.zeros((lanes,), F32)
        for kk in range(d // lanes):
            out_v[pl.ds(kk * lanes, lanes)] = zero
        gather(0, 0).start()

        @pl.loop(0, nchunk // 2)
        def _(c2):
            for b in range(2):
                c = c2 * 2 + b
                if b == 0:
                    gather(c + 1, 1).start()
                else:
                    @pl.when(c2 < nchunk // 2 - 1)
                    def _():
                        gather(c + 1, 0).start()
                gather(c, b).wait()
                compute(c, bufs[b][0])

        pltpu.sync_copy(out_v, o_hbm.at[t])


def _experts_sc(e, w, h, table, *, start, count):
    nsel = e.shape[1]
    d = h.shape[1]
    assert count % SC_WORKERS == 0 and nsel % (2 * SC_LANES) == 0
    mesh = plsc.VectorSubcoreMesh(core_axis_name="c", subcore_axis_name="s")
    return pl.kernel(
        functools.partial(_sc_expert_kernel, start=start, per_worker=count // SC_WORKERS),
        mesh=mesh,
        compiler_params=pltpu.CompilerParams(needs_layout_passes=False),
        out_type=jax.ShapeDtypeStruct((count, d), F32),
        scratch_types=[pltpu.VMEM((nsel,), I32), pltpu.VMEM((nsel,), F32),
                       pltpu.VMEM((d,), F32), pltpu.VMEM((d,), F32),
                       pltpu.VMEM((SC_LANES, d), U32), pltpu.VMEM((SC_LANES, d), U32),
                       pltpu.SemaphoreType.DMA, pltpu.SemaphoreType.DMA],
    )(e, w, h, table)


def _residual_kernel(x_ref, p_ref, g_ref, fg_ref, base_ref, o_ref, *, final_norm):
    del base_ref
    y = x_ref[0] + g_ref[0] * p_ref[...]
    if final_norm:
        ms = jnp.mean(y * y, axis=-1, keepdims=True)
        y = y * lax.rsqrt(ms + EPS) * fg_ref[...]
    o_ref[0] = y


def _residual(x, peer, gate, final_g, base, *, start, per_batch, final_norm):
    count, d = peer.shape
    tb = TOKEN_BLOCK
    assert count % tb == 0 and start % tb == 0
    first = start // tb
    return pl.pallas_call(
        functools.partial(_residual_kernel, final_norm=final_norm),
        grid=(count // tb,),
        in_specs=[pl.BlockSpec((1, tb, d), lambda i: (0, first + i, 0)),
                  pl.BlockSpec((tb, d), lambda i: (i, 0)),
                  pl.BlockSpec((1, 1, d), lambda i: ((first + i) // (per_batch // tb), 0, 0)),
                  pl.BlockSpec((1, d), lambda i: (0, 0)),
                  pl.BlockSpec(memory_space=pl.ANY)],
        out_specs=pl.BlockSpec((1, tb, d), lambda i: (0, first + i, 0)),
        out_shape=jax.ShapeDtypeStruct(base.shape, F32),
        input_output_aliases={4: 0},
        compiler_params=_params("arbitrary"),
        name="peer_residual",
    )(x, peer, gate, final_g.reshape(1, d), base)


def _mixer(x, mods, width, h0, prm, *, need_output=True):
    sh1, sc1, g1 = mods
    proj = _norm_mod_matmul(x, prm["norm1_g"], sh1, sc1, prm["w_in"])
    rnn = (prm["conv_b_w"], prm["conv_b_b"])
    hf, st_f = _lru_scan(proj, *rnn, prm["wcat"][0], prm["ba"][0], prm["bx"][0], prm["lam"][0], h0[0],
                         reverse=False, width=width)
    if not need_output:
        _, st_b = _lru_scan(proj, *rnn, prm["wcat"][1], prm["ba"][1], prm["bx"][1], prm["lam"][1], h0[1],
                            reverse=True, width=width)
        return None, (st_f, st_b)
    zb, st_b = _lru_scan(proj, *rnn, prm["wcat"][1], prm["ba"][1], prm["bx"][1], prm["lam"][1], h0[1],
                         reverse=True, width=width, hf=hf)
    merged = _merge(proj, zb, prm["conv_a_w"], prm["conv_a_b"], prm["w_pa"], prm["w_pb"], width=width)
    return _proj_residual(merged, prm["w_o"], x, g1), (st_f, st_b)


def _peer(x, mods, prm, final_g, *, final_norm=False):
    sh2, sc2, g2 = mods
    bsz, t, d = x.shape
    n = bsz * t
    xf = x.reshape(1, n, d)
    assert ROW_TILE % TOKEN_BLOCK == 0 and TOKEN_BLOCK % SC_WORKERS == 0
    n_sc = int(n * SC_SHARE) // ROW_TILE * ROW_TILE
    n_tc = n - n_sc

    def route(start, count):
        q, h = _norm_mod_matmul_rows(xf, prm["norm2_g"], sh2, sc2, prm["wq"],
                                     start=start, count=count, per_batch=t)
        e, w = _route(q, prm["k1"], prm["k2"])
        return e, w, h

    if n_sc:
        e, w, h = route(n_tc, n_sc)
        peer_sc = _experts_sc(e[0], w[0], h[0], prm["table"], start=0, count=n_sc)
    e, w, h = route(0, n_tc)
    out = _experts(e, w, h, xf, g2, final_g, prm["table"], n_tok=n_tc, per_batch=t, final_norm=final_norm)
    if n_sc:
        out = _residual(xf, peer_sc, g2, final_g, out, start=n_tc, per_batch=t, final_norm=final_norm)
    return out.reshape(bsz, t, d)


def kernel(x, c, ctx, c_ctx, ada_w, ada_b, norm1_g, norm2_g, w_in, conv_a_w, conv_a_b, conv_b_w, conv_b_b,
           lru_wa, lru_ba, lru_wx, lru_bx, lru_lam, w_pa, w_pb, w_o, peer_wq, peer_k1, peer_k2, peer_u,
           peer_v, final_g):
    depth = ada_w.shape[0]
    bsz, _, d = x.shape
    ctx_len = ctx.shape[1]
    cwid = conv_a_w.shape[2]
    rw = conv_b_w.shape[2]
    assert rw == d and (2 * rw + 2 * d) % cwid == 0 and bsz + 1 <= 8

    cond = jnp.zeros((8, d), F32).at[:bsz].set(c).at[bsz].set(c_ctx)
    zero_state = jnp.zeros((2, bsz, 1, rw), F32)
    s3 = 3 * cwid

    for l in range(depth):
        w_in_l = w_in[l]
        prm = dict(
            norm1_g=norm1_g[l], norm2_g=norm2_g[l],
            w_in=jnp.concatenate([w_in_l[:, s3:], w_in_l[:, :s3]], axis=1).astype(BF16),
            conv_a_w=conv_a_w[l], conv_a_b=conv_a_b[l], conv_b_w=conv_b_w[l], conv_b_b=conv_b_b[l],
            wcat=jnp.concatenate([lru_wa[l], lru_wx[l]], axis=-1).astype(BF16),
            ba=lru_ba[l], bx=lru_bx[l], lam=lru_lam[l],
            w_pa=w_pa[l].astype(BF16), w_pb=w_pb[l].astype(BF16), w_o=w_o[l].astype(BF16),
            wq=peer_wq[l].astype(BF16), k1=peer_k1[l].astype(BF16), k2=peer_k2[l].astype(BF16),
            table=_pack_tables(peer_u[l], peer_v[l]),
        )
        m = _adaln(cond, ada_w[l], ada_b[l])
        lat = [m[:bsz, k * d:(k + 1) * d].reshape(bsz, 1, d) for k in range(6)]
        cmod = [jnp.broadcast_to(m[bsz, k * d:(k + 1) * d].reshape(1, 1, d), (bsz, 1, d)) for k in range(6)]
        last = l == depth - 1

        ctx_mix, ctx_state = _mixer(ctx, cmod[0:3], ctx_len, zero_state, prm, need_output=not last)
        if not last:
            ctx = _peer(ctx_mix, cmod[3:6], prm, final_g)
        x = _mixer(x, lat[0:3], GRID_W, ctx_state, prm)[0]
        x = _peer(x, lat[3:6], prm, final_g, final_norm=last)
    return x
```

```python
import functools

import jax
import jax.numpy as jnp
from jax import lax
from jax.experimental import pallas as pl
from jax.experimental.pallas import tpu as pltpu
from jax.experimental.pallas import tpu_sc as plsc

GRID_W = 64
TOPK = 16
LRU_C = 8.0
EPS = 1e-6
VMEM_LIMIT_BYTES = 56 * 1024 * 1024
LANES = 128

F32 = jnp.float32
BF16 = jnp.bfloat16
U32 = jnp.uint32
I32 = jnp.int32


def _params(*sem):
    return pltpu.CompilerParams(dimension_semantics=sem, vmem_limit_bytes=VMEM_LIMIT_BYTES)


def _resident(shape):
    nd = len(shape)
    return pl.BlockSpec(shape, lambda *_: (0,) * nd, pipeline_mode=pl.Buffered(1))


def _tile(n, target):
    if n <= target:
        return n
    t = target - target % LANES
    while n % t:
        t -= LANES
    return t


def _gelu(x):
    return 0.5 * x * (1.0 + jnp.tanh(0.7978845608028654 * (x + 0.044715 * x * x * x)))


def _sigmoid(x):
    return 1.0 / (1.0 + jnp.exp(-x))


def _adaln_kernel(cond_ref, w_ref, b_ref, o_ref):
    cnd = cond_ref[...]
    s = (cnd * _sigmoid(cnd)).astype(BF16)
    o_ref[...] = jnp.dot(s, w_ref[...].astype(BF16), preferred_element_type=F32) + b_ref[...]


def _adaln(cond, w, b):
    rows, d = cond.shape
    n = w.shape[1]
    tn = _tile(n, 1024)
    return pl.pallas_call(
        _adaln_kernel,
        grid=(n // tn,),
        in_specs=[pl.BlockSpec((rows, d), lambda j: (0, 0)),
                  pl.BlockSpec((d, tn), lambda j: (0, j)),
                  pl.BlockSpec((1, tn), lambda j: (0, j))],
        out_specs=pl.BlockSpec((rows, tn), lambda j: (0, j)),
        out_shape=jax.ShapeDtypeStruct((rows, n), F32),
        compiler_params=_params("arbitrary"),
        name="adaln",
    )(cond, w, b.reshape(1, n))


def _nmm_kernel(x_ref, g_ref, sh_ref, sc_ref, w_ref, o_ref, *rest, emit_h):
    if emit_h:
        h_out_ref, h_scr = rest
    else:
        (h_scr,) = rest

    @pl.when(pl.program_id(2) == 0)
    def _():
        x = x_ref[0]
        ms = jnp.mean(x * x, axis=-1, keepdims=True)
        y = x * lax.rsqrt(ms + EPS) * g_ref[...]
        h = y * (1.0 + sc_ref[0]) + sh_ref[0]
        h_scr[...] = h.astype(BF16)
        if emit_h:
            h_out_ref[0] = h

    o_ref[0] = jnp.dot(h_scr[...], w_ref[...], preferred_element_type=F32)


def _norm_mod_matmul(x, g, shift, scale, w, *, emit_h=False):
    bsz, t, d = x.shape
    n = w.shape[1]
    tm = min(512 if emit_h else 1024, t)
    tn = _tile(n, 1024)
    out_shape = [jax.ShapeDtypeStruct((bsz, t, n), F32)]
    out_specs = [pl.BlockSpec((1, tm, tn), lambda b, i, j: (b, i, j))]
    if emit_h:
        out_shape.append(jax.ShapeDtypeStruct((bsz, t, d), F32))
        out_specs.append(pl.BlockSpec((1, tm, d), lambda b, i, j: (b, i, 0)))
    res = pl.pallas_call(
        functools.partial(_nmm_kernel, emit_h=emit_h),
        grid=(bsz, t // tm, n // tn),
        in_specs=[pl.BlockSpec((1, tm, d), lambda b, i, j: (b, i, 0)),
                  pl.BlockSpec((1, d), lambda b, i, j: (0, 0)),
                  pl.BlockSpec((1, 1, d), lambda b, i, j: (b, 0, 0)),
                  pl.BlockSpec((1, 1, d), lambda b, i, j: (b, 0, 0)),
                  pl.BlockSpec((d, tn), lambda b, i, j: (0, j))],
        out_specs=out_specs,
        out_shape=out_shape,
        scratch_shapes=[pltpu.VMEM((tm, d), BF16)],
        compiler_params=_params("arbitrary", "arbitrary", "arbitrary"),
        name="norm_mod_matmul",
    )(x, g.reshape(1, d), shift, scale, w)
    return res if emit_h else res[0]


ROW_TILE = 512


def _norm_mod_matmul_rows(xf, g, shift, scale, w, *, start, count, per_batch):
    d = xf.shape[2]
    n = w.shape[1]
    tm = min(ROW_TILE, per_batch)
    tn = _tile(n, 1024)
    assert start % tm == 0 and count % tm == 0 and per_batch % tm == 0
    first, per = start // tm, per_batch // tm
    return pl.pallas_call(
        functools.partial(_nmm_kernel, emit_h=True),
        grid=(1, count // tm, n // tn),
        in_specs=[pl.BlockSpec((1, tm, d), lambda b, i, j: (0, first + i, 0)),
                  pl.BlockSpec((1, d), lambda b, i, j: (0, 0)),
                  pl.BlockSpec((1, 1, d), lambda b, i, j: ((first + i) // per, 0, 0)),
                  pl.BlockSpec((1, 1, d), lambda b, i, j: ((first + i) // per, 0, 0)),
                  pl.BlockSpec((d, tn), lambda b, i, j: (0, j))],
        out_specs=[pl.BlockSpec((1, tm, tn), lambda b, i, j: (0, i, j)),
                   pl.BlockSpec((1, tm, d), lambda b, i, j: (0, i, 0))],
        out_shape=[jax.ShapeDtypeStruct((1, count, n), F32), jax.ShapeDtypeStruct((1, count, d), F32)],
        scratch_shapes=[pltpu.VMEM((tm, d), BF16)],
        compiler_params=_params("arbitrary", "arbitrary", "arbitrary"),
        name="norm_mod_matmul",
    )(xf, g.reshape(1, d), shift, scale, w)


def _row_shift(x, off, pos, width):
    if off == 0:
        return x
    n = x.shape[0]
    r = pltpu.roll(x, (-off) % n, axis=0)
    valid = jnp.logical_and(pos + off >= 0, pos + off < width)
    return jnp.where(valid, r, 0.0)


def _tile_scan(a, u, state, reverse):
    tm, c = a.shape
    ngrp = tm // SUBLANES
    a3 = a.reshape(ngrp, SUBLANES, c)
    u3 = u.reshape(ngrp, SUBLANES, c)
    sub = lax.broadcasted_iota(I32, (1, SUBLANES, 1), 1)
    for d in (1, 2, 4):
        shift = SUBLANES - d if reverse else d
        keep = sub < SUBLANES - d if reverse else sub >= d
        a_s = pltpu.roll(a3, shift, axis=1)
        u_s = pltpu.roll(u3, shift, axis=1)
        u3 = jnp.where(keep, u3 + a3 * u_s, u3)
        a3 = jnp.where(keep, a3 * a_s, a3)
    hs = [None] * ngrp
    for g in (range(ngrp - 1, -1, -1) if reverse else range(ngrp)):
        hg = u3[g] + a3[g] * state
        hs[g] = hg
        state = hg[0:1, :] if reverse else hg[SUBLANES - 1:SUBLANES, :]
    return jnp.concatenate(hs, axis=0), state


def _scan_kernel(*refs, reverse, width, final, nblk):
    if final:
        (xr_ref, cw_ref, cb_ref, w_ref, ba_ref, bx_ref, lam_ref, h0_ref, hf_ref, gr_ref,
         o_ref, st_ref, carry) = refs
    else:
        (xr_ref, cw_ref, cb_ref, w_ref, ba_ref, bx_ref, lam_ref, h0_ref,
         o_ref, st_ref, carry) = refs

    @pl.when(pl.program_id(1) == 0)
    def _():
        carry[...] = h0_ref[0]

    x = xr_ref[0]
    tm, rw = x.shape
    bs = rw // nblk
    row = lax.broadcasted_iota(I32, (tm, 1), 0)
    pos = row % width
    cw = cw_ref[...]
    xc = cb_ref[...] + sum(cw[k:k + 1, :] * _row_shift(x, k - 2, pos, width) for k in range(4))

    for n in range(nblk):
        cols = slice(n * bs, (n + 1) * bs)
        xn = xc[:, cols]
        pre = jnp.dot(xn.astype(BF16), w_ref[n], preferred_element_type=F32)
        r = _sigmoid(pre[:, :bs] + ba_ref[:, cols])
        ig = _sigmoid(pre[:, bs:] + bx_ref[:, cols])
        z = -lam_ref[:, cols]
        sp = jnp.maximum(z, 0.0) + jnp.log(1.0 + jnp.exp(-jnp.abs(z)))
        log_a = -LRU_C * r * sp
        a = jnp.exp(log_a)
        u = jnp.sqrt(1.0 - a * a) * (ig * xn)
        h, last = _tile_scan(a, u, carry[:, cols], reverse)
        carry[:, cols] = last
        if final:
            zb = (hf_ref[0, :, cols] + h) * _gelu(gr_ref[0, :, cols])
            o_ref[0, :, cols] = zb.astype(o_ref.dtype)
        else:
            o_ref[0, :, cols] = h
    st_ref[0] = carry[...]


def _lru_scan(proj, conv_w, conv_b, wcat, ba, bx, lam, h0, *, reverse, width, hf=None):
    bsz, t, _ = proj.shape
    rw = conv_w.shape[1]
    nblk = wcat.shape[0]
    final = hf is not None
    tm = min(256, t)
    nt = t // tm
    tix = (lambda i: nt - 1 - i) if reverse else (lambda i: i)
    vec = lambda a: a.reshape(1, rw)
    in_specs = [pl.BlockSpec((1, tm, rw), lambda b, i: (b, tix(i), 0)),
                _resident((4, rw)), _resident((1, rw)), _resident(wcat.shape),
                _resident((1, rw)), _resident((1, rw)), _resident((1, rw)),
                pl.BlockSpec((1, 1, rw), lambda b, i: (b, 0, 0))]
    args = [proj, conv_w, vec(conv_b), wcat, vec(ba), vec(bx), vec(lam), h0]
    if final:
        in_specs += [pl.BlockSpec((1, tm, rw), lambda b, i: (b, tix(i), 0)),
                     pl.BlockSpec((1, tm, rw), lambda b, i: (b, tix(i), 1))]
        args += [hf, proj]
    out, state = pl.pallas_call(
        functools.partial(_scan_kernel, reverse=reverse, width=width, final=final, nblk=nblk),
        grid=(bsz, nt),
        in_specs=in_specs,
        out_specs=[pl.BlockSpec((1, tm, rw), lambda b, i: (b, tix(i), 0)),
                   pl.BlockSpec((1, 1, rw), lambda b, i: (b, 0, 0))],
        out_shape=[jax.ShapeDtypeStruct((bsz, t, rw), BF16 if final else F32),
                   jax.ShapeDtypeStruct((bsz, 1, rw), F32)],
        scratch_shapes=[pltpu.VMEM((1, rw), F32)],
        compiler_params=_params("arbitrary", "arbitrary"),
        name="lru_scan_bwd" if reverse else "lru_scan_fwd",
    )(*args)
    return out, state


def _merge_kernel(ga_ref, gb_ref, ba_ref, ca_ref, va_ref, zb_ref, cw_ref, cb_ref, wpa_ref, wpb_ref,
                  o_ref, *, width):
    cv = ca_ref[0] * va_ref[0]
    tm = cv.shape[0]
    pos = lax.broadcasted_iota(I32, (tm, 1), 0) % width
    cw = cw_ref[...]
    conv = cb_ref[...] + sum(cw[k:k + 1, :] * _row_shift(cv, k - 1, pos, width) for k in range(3))
    za = (ba_ref[0] * conv).astype(BF16)
    ya = jnp.dot(za, wpa_ref[...], preferred_element_type=F32)
    yb = jnp.dot(zb_ref[0], wpb_ref[...], preferred_element_type=F32)
    o_ref[0] = (_sigmoid(ga_ref[0]) * ya + _sigmoid(gb_ref[0]) * yb).astype(o_ref.dtype)


def _merge(proj, zb, conv_w, conv_b, w_pa, w_pb, *, width):
    bsz, t, _ = proj.shape
    cwid, d = w_pa.shape
    rw = w_pb.shape[0]
    tm = min(256, t)
    ga_blk = (2 * rw) // d
    ca_blk = (2 * rw + 2 * d) // cwid
    pblk = lambda wdt, k: pl.BlockSpec((1, tm, wdt), lambda b, i: (b, i, k))
    return pl.pallas_call(
        functools.partial(_merge_kernel, width=width),
        grid=(bsz, t // tm),
        in_specs=[pblk(d, ga_blk), pblk(d, ga_blk + 1),
                  pblk(cwid, ca_blk), pblk(cwid, ca_blk + 1), pblk(cwid, ca_blk + 2),
                  pl.BlockSpec((1, tm, rw), lambda b, i: (b, i, 0)),
                  _resident((3, cwid)), _resident((1, cwid)),
                  _resident((cwid, d)), _resident((rw, d))],
        out_specs=pl.BlockSpec((1, tm, d), lambda b, i: (b, i, 0)),
        out_shape=jax.ShapeDtypeStruct((bsz, t, d), BF16),
        compiler_params=_params("arbitrary", "arbitrary"),
        name="merge",
    )(proj, proj, proj, proj, proj, zb, conv_w, conv_b.reshape(1, cwid), w_pa, w_pb)


def _proj_res_kernel(m_ref, w_ref, x_ref, g_ref, o_ref):
    y = jnp.dot(m_ref[0], w_ref[...], preferred_element_type=F32)
    o_ref[0] = x_ref[0] + g_ref[0] * y


def _proj_residual(m, w, x, gate):
    bsz, t, d = x.shape
    k = m.shape[2]
    tm = min(512, t)
    return pl.pallas_call(
        _proj_res_kernel,
        grid=(bsz, t // tm),
        in_specs=[pl.BlockSpec((1, tm, k), lambda b, i: (b, i, 0)),
                  _resident((k, d)),
                  pl.BlockSpec((1, tm, d), lambda b, i: (b, i, 0)),
                  pl.BlockSpec((1, 1, d), lambda b, i: (b, 0, 0))],
        out_specs=pl.BlockSpec((1, tm, d), lambda b, i: (b, i, 0)),
        out_shape=jax.ShapeDtypeStruct((bsz, t, d), F32),
        compiler_params=_params("arbitrary", "arbitrary"),
        name="proj_residual",
    )(m, w, x, gate)


def _top16(s, payload=None):
    iota = lax.broadcasted_iota(I32, s.shape, 0).astype(F32)
    vals, picks = [], []
    for _ in range(TOPK):
        m = jnp.max(s, axis=0, keepdims=True)
        idx = jnp.min(jnp.where(s == m, iota, float(s.shape[0])), axis=0, keepdims=True)
        hit = iota == idx
        vals.append(m)
        if payload is None:
            picks.append(idx)
        else:
            picks.append(jnp.sum(jnp.where(hit, payload, 0.0), axis=0, keepdims=True))
        s = jnp.where(hit, -jnp.inf, s)
    return jnp.concatenate(vals, axis=0), jnp.concatenate(picks, axis=0)


def _pair_candidates(v1, i1, v2, i2, keys):
    sub = 8
    row = lax.broadcasted_iota(I32, (sub, 1), 0)
    cand = [v1[0:1] + v2]
    ecand = [i1[0:1] * keys + i2]
    for a in range(1, sub):
        ok = row < TOPK // (a + 1)
        cand.append(jnp.where(ok, v1[a:a + 1] + v2[0:sub], -jnp.inf))
        ecand.append(i1[a:a + 1] * keys + i2[0:sub])
    cand.append(v1[sub:] + v2[0:1])
    ecand.append(i1[sub:] * keys + i2[0:1])
    return jnp.concatenate(cand, axis=0), jnp.concatenate(ecand, axis=0)


def _route_kernel(q_ref, k1_ref, k2_ref, e_ref, w_ref, *, heads, keys):
    q = q_ref[0].astype(BF16)
    half = k1_ref.shape[2]
    dn = (((1,), (1,)), ((), ()))
    es, ws = [], []
    for h in range(heads):
        q1 = q[:, h * 2 * half: h * 2 * half + half]
        q2 = q[:, h * 2 * half + half: (h + 1) * 2 * half]
        s1 = lax.dot_general(k1_ref[h], q1, dn, preferred_element_type=F32)
        s2 = lax.dot_general(k2_ref[h], q2, dn, preferred_element_type=F32)
        v1, i1 = _top16(s1)
        v2, i2 = _top16(s2)
        score, e = _top16(*_pair_candidates(v1, i1, v2, i2, float(keys)))
        p = jnp.exp(score - score[0:1, :])
        ws.append(p / jnp.sum(p, axis=0, keepdims=True))
        es.append(e)
    e_ref[0] = jnp.concatenate(es, axis=0).T.astype(I32)
    w_ref[0] = jnp.concatenate(ws, axis=0).T


def _route(q, k1, k2):
    bsz, t, hq = q.shape
    heads, keys, _ = k1.shape
    tb = min(256, t)
    nsel = heads * TOPK
    return pl.pallas_call(
        functools.partial(_route_kernel, heads=heads, keys=keys),
        grid=(bsz, t // tb),
        in_specs=[pl.BlockSpec((1, tb, hq), lambda b, i: (b, i, 0)),
                  _resident(k1.shape), _resident(k2.shape)],
        out_specs=[pl.BlockSpec((1, tb, nsel), lambda b, i: (b, i, 0)),
                   pl.BlockSpec((1, tb, nsel), lambda b, i: (b, i, 0))],
        out_shape=[jax.ShapeDtypeStruct((bsz, t, nsel), I32),
                   jax.ShapeDtypeStruct((bsz, t, nsel), F32)],
        compiler_params=_params("arbitrary", "arbitrary"),
        name="peer_route",
    )(q, k1, k2)


def _pack_kernel(u_ref, v_ref, o_ref):
    half = u_ref.shape[1] // 2

    def pack(x):
        bits = lax.bitcast_convert_type(x.astype(BF16).astype(F32), U32)
        return (bits[:, :half] >> 16) | (bits[:, half:] & jnp.uint32(0xFFFF0000))

    o_ref[:, :half] = pack(u_ref[...])
    o_ref[:, half:] = pack(v_ref[...])


def _pack_tables(u, v):
    e, d = u.shape
    te = 256
    return pl.pallas_call(
        _pack_kernel,
        grid=(e // te,),
        in_specs=[pl.BlockSpec((te, d), lambda i: (i, 0)), pl.BlockSpec((te, d), lambda i: (i, 0))],
        out_specs=pl.BlockSpec((te, d), lambda i: (i, 0)),
        out_shape=jax.ShapeDtypeStruct((e, d), U32),
        compiler_params=_params("arbitrary"),
        name="pack_tables",
    )(u, v)


def _unpack(words):
    lo = lax.bitcast_convert_type(words << 16, F32)
    hi = lax.bitcast_convert_type(words & jnp.uint32(0xFFFF0000), F32)
    return lo, hi


NSLOT = 4
LOOKAHEAD = NSLOT - 1
SUBLANES = 8


def _expert_kernel(e_ref, w_ref, h_ref, x_ref, g_ref, fg_ref, tab_ref, o_ref, *scratch, final_norm):
    bufs, sems = scratch[:NSLOT], scratch[NSLOT]
    tb, nsel = w_ref.shape[1], w_ref.shape[2]
    d = h_ref.shape[2]
    half = d // 2
    ngrp = nsel // SUBLANES

    def row_copy(t, j, slot):
        return pltpu.make_async_copy(tab_ref.at[pl.ds(e_ref[0, t, j], 1), :],
                                     bufs[slot].at[pl.ds(j, 1), :], sems.at[slot])

    lane = lax.broadcasted_iota(I32, (SUBLANES, nsel), 1)
    sub = lax.broadcasted_iota(I32, (SUBLANES, nsel), 0)

    def token(t, slot, t_next):
        for j in range(nsel):
            row_copy(t, j, slot).wait()
        buf = bufs[slot]
        nslot = (slot + LOOKAHEAD) % NSLOT
        hrow = h_ref[0, pl.ds(t, 1), :]
        xlo = jnp.broadcast_to(hrow[:, :half], (SUBLANES, half))
        xhi = jnp.broadcast_to(hrow[:, half:], (SUBLANES, half))
        wrow = w_ref[0, pl.ds(t, 1), :]
        acc_lo = jnp.zeros((SUBLANES, half), F32)
        acc_hi = jnp.zeros((SUBLANES, half), F32)
        for g in range(ngrp):
            rows = slice(g * SUBLANES, (g + 1) * SUBLANES)
            ulo, uhi = _unpack(buf[rows, :half])
            act = jnp.sum(ulo * xlo + uhi * xhi, axis=1, keepdims=True)
            wcol = jnp.sum(jnp.where(lane == sub + g * SUBLANES, wrow, 0.0), axis=1, keepdims=True)
            coef = wcol * _gelu(act)
            vlo, vhi = _unpack(buf[rows, half:])
            acc_lo = acc_lo + coef * vlo
            acc_hi = acc_hi + coef * vhi
            if t_next is not None:
                for j in range(g * SUBLANES, (g + 1) * SUBLANES):
                    row_copy(t_next, j, nslot).start()
        gate = g_ref[0]
        olo = jnp.sum(acc_lo, axis=0, keepdims=True)
        ohi = jnp.sum(acc_hi, axis=0, keepdims=True)
        o_ref[0, pl.ds(t, 1), :half] = x_ref[0, pl.ds(t, 1), :half] + gate[:, :half] * olo
        o_ref[0, pl.ds(t, 1), half:] = x_ref[0, pl.ds(t, 1), half:] + gate[:, half:] * ohi

    for t in range(LOOKAHEAD):
        for j in range(nsel):
            row_copy(t, j, t).start()

    def body(q, c):
        for s in range(NSLOT):
            t = q * NSLOT + s
            token(t, s, t + LOOKAHEAD)
        return c

    lax.fori_loop(0, tb // NSLOT - 1, body, 0)
    for s in range(NSLOT):
        t = tb - NSLOT + s
        token(t, s, t + LOOKAHEAD if s == 0 else None)

    if final_norm:
        y = o_ref[0]
        ms = jnp.mean(y * y, axis=-1, keepdims=True)
        o_ref[0] = y * lax.rsqrt(ms + EPS) * fg_ref[...]


TOKEN_BLOCK = 256


def _experts(e, w, h, x, gate, final_g, table, *, n_tok, per_batch, final_norm):
    d = x.shape[2]
    nsel = e.shape[2]
    tb = TOKEN_BLOCK
    assert tb % NSLOT == 0 and nsel % SUBLANES == 0 and n_tok % tb == 0 and per_batch % tb == 0
    tok = lambda wdt: pl.BlockSpec((1, tb, wdt), lambda i: (0, i, 0))
    return pl.pallas_call(
        functools.partial(_expert_kernel, final_norm=final_norm),
        grid=(n_tok // tb,),
        in_specs=[pl.BlockSpec((1, tb, nsel), lambda i: (0, i, 0), memory_space=pltpu.SMEM),
                  tok(nsel), tok(d), tok(d),
                  pl.BlockSpec((1, 1, d), lambda i: (i // (per_batch // tb), 0, 0)),
                  pl.BlockSpec((1, d), lambda i: (0, 0)),
                  pl.BlockSpec(memory_space=pl.ANY)],
        out_specs=tok(d),
        out_shape=jax.ShapeDtypeStruct(x.shape, F32),
        scratch_shapes=[pltpu.VMEM((nsel, d), U32)] * NSLOT + [pltpu.SemaphoreType.DMA((NSLOT,))],
        compiler_params=_params("arbitrary"),
        name="peer_experts",
    )(e, w, h, x, gate, final_g.reshape(1, d), table)


SC_LANES = 16
SC_WORKERS = 32
SC_SHARE = 0.547


def _sc_expert_kernel(e_hbm, w_hbm, h_hbm, tab_hbm, o_hbm, idx_v, w_v, h_v, out_v, rows_a, rows_b,
                      sem_a, sem_b, sem_i0, sem_i1, sem_o0, sem_o1, *, per_worker):
    lanes = SC_LANES
    nsel = idx_v.shape[1]
    d = h_v.shape[1]
    half = d // 2
    npair = nsel // lanes // 2
    base = (lax.axis_index("s") * 2 + lax.axis_index("c")) * per_worker
    lane = lax.iota(I32, lanes)
    bufs = ((rows_a, sem_a), (rows_b, sem_b))
    sem_in = (sem_i0, sem_i1)
    sem_out = (sem_o0, sem_o1)

    def in_copies(t, p):
        return (pltpu.make_async_copy(e_hbm.at[t], idx_v.at[p], sem_in[p]),
                pltpu.make_async_copy(w_hbm.at[t], w_v.at[p], sem_in[p]),
                pltpu.make_async_copy(h_hbm.at[t], h_v.at[p], sem_in[p]))

    def out_copy(t, p):
        return pltpu.make_async_copy(out_v.at[p], o_hbm.at[t], sem_out[p])

    def gather(p, c, b):
        rows, sem = bufs[b]
        return pltpu.make_async_copy(tab_hbm.at[idx_v.at[p, pl.ds(c * lanes, lanes)]], rows, sem)

    def compute(p, c, rows, first):
        w16 = w_v[p, pl.ds(c * lanes, lanes)]

        def ublock(kb, accs):
            off = pl.multiple_of(kb * lanes, lanes)
            hl = h_v[p, pl.ds(off, lanes)]
            hh = h_v[p, pl.ds(half + off, lanes)]
            out = []
            for r in range(lanes):
                lo, hi = _unpack(rows[r, pl.ds(off, lanes)])
                out.append(accs[r] + lo * hl + hi * hh)
            return tuple(out)

        zeros = tuple(jnp.zeros((lanes,), F32) for _ in range(lanes))
        accs = plsc.parallel_loop(0, half // lanes, carry=zeros)(ublock)
        act = jnp.zeros((lanes,), F32)
        for r in range(lanes):
            act = jnp.where(lane == r, jnp.sum(accs[r]), act)
        z = 0.7978845608028654 * (act + 0.044715 * act * act * act)
        th = 1.0 - 2.0 / (1.0 + jnp.exp(2.0 * z))
        coef16 = w16 * (0.5 * act * (1.0 + th))
        coefs = [jnp.take(coef16, jnp.full((lanes,), r, I32)) for r in range(lanes)]

        def vblock(kb, carry):
            off = pl.multiple_of(kb * lanes, lanes)
            if first:
                al = jnp.zeros((lanes,), F32)
                ah = jnp.zeros((lanes,), F32)
            else:
                al = out_v[p, pl.ds(off, lanes)]
                ah = out_v[p, pl.ds(half + off, lanes)]
            for r in range(lanes):
                lo, hi = _unpack(rows[r, pl.ds(half + off, lanes)])
                al = al + coefs[r] * lo
                ah = ah + coefs[r] * hi
            out_v[p, pl.ds(off, lanes)] = al
            out_v[p, pl.ds(half + off, lanes)] = ah
            return carry

        plsc.parallel_loop(0, half // lanes, carry=jnp.int32(0))(vblock)

    for cp in in_copies(base, 0):
        cp.start()
    for cp in in_copies(base, 0):
        cp.wait()
    gather(0, 0, 0).start()

    @pl.loop(0, per_worker // 2)
    def _(i2):
        for p in range(2):
            i = i2 * 2 + p
            t = base + i
            more = i + 1 < per_worker

            @pl.when(more)
            def _():
                for cp in in_copies(t + 1, 1 - p):
                    cp.start()

            @pl.when(i >= 2)
            def _():
                out_copy(t - 2, p).wait()

            gather(p, 1, 1).start()
            gather(p, 0, 0).wait()
            compute(p, 0, rows_a, True)
            gather(p, 2, 0).start()
            gather(p, 1, 1).wait()
            compute(p, 1, rows_b, False)

            @pl.loop(1, npair)
            def _(c2):
                c = c2 * 2
                gather(p, c + 1, 1).start()
                gather(p, c, 0).wait()
                compute(p, c, rows_a, False)

                @pl.when(c2 < npair - 1)
                def _():
                    gather(p, c + 2, 0).start()

                @pl.when(jnp.logical_and(c2 == npair - 1, more))
                def _():
                    for cp in in_copies(t + 1, 1 - p):
                        cp.wait()
                    gather(1 - p, 0, 0).start()

                gather(p, c + 1, 1).wait()
                compute(p, c + 1, rows_b, False)

            out_copy(t, p).start()

    out_copy(base + per_worker - 2, 0).wait()
    out_copy(base + per_worker - 1, 1).wait()


def _experts_sc(e, w, h, table):
    count, nsel = e.shape
    d = h.shape[1]
    assert count % (2 * SC_WORKERS) == 0 and nsel // SC_LANES >= 4 and nsel % (2 * SC_LANES) == 0
    mesh = plsc.VectorSubcoreMesh(core_axis_name="c", subcore_axis_name="s")
    dma = pltpu.SemaphoreType.DMA
    return pl.kernel(
        functools.partial(_sc_expert_kernel, per_worker=count // SC_WORKERS),
        mesh=mesh,
        compiler_params=pltpu.CompilerParams(needs_layout_passes=False),
        out_type=jax.ShapeDtypeStruct((count, d), F32),
        scratch_types=[pltpu.VMEM((2, nsel), I32), pltpu.VMEM((2, nsel), F32),
                       pltpu.VMEM((2, d), F32), pltpu.VMEM((2, d), F32),
                       pltpu.VMEM((SC_LANES, d), U32), pltpu.VMEM((SC_LANES, d), U32),
                       dma, dma, dma, dma, dma, dma],
    )(e, w, h, table)


def _residual_kernel(x_ref, p_ref, g_ref, fg_ref, base_ref, o_ref, *, final_norm):
    del base_ref
    y = x_ref[0] + g_ref[0] * p_ref[...]
    if final_norm:
        ms = jnp.mean(y * y, axis=-1, keepdims=True)
        y = y * lax.rsqrt(ms + EPS) * fg_ref[...]
    o_ref[0] = y


def _residual(x, peer, gate, final_g, base, *, start, per_batch, final_norm):
    count, d = peer.shape
    tb = TOKEN_BLOCK
    assert count % tb == 0 and start % tb == 0
    first = start // tb
    return pl.pallas_call(
        functools.partial(_residual_kernel, final_norm=final_norm),
        grid=(count // tb,),
        in_specs=[pl.BlockSpec((1, tb, d), lambda i: (0, first + i, 0)),
                  pl.BlockSpec((tb, d), lambda i: (i, 0)),
                  pl.BlockSpec((1, 1, d), lambda i: ((first + i) // (per_batch // tb), 0, 0)),
                  pl.BlockSpec((1, d), lambda i: (0, 0)),
                  pl.BlockSpec(memory_space=pl.ANY)],
        out_specs=pl.BlockSpec((1, tb, d), lambda i: (0, first + i, 0)),
        out_shape=jax.ShapeDtypeStruct(base.shape, F32),
        input_output_aliases={4: 0},
        compiler_params=_params("arbitrary"),
        name="peer_residual",
    )(x, peer, gate, final_g.reshape(1, d), base)


def _mixer(x, mods, width, h0, prm, *, need_output=True):
    sh1, sc1, g1 = mods
    proj = _norm_mod_matmul(x, prm["norm1_g"], sh1, sc1, prm["w_in"])
    rnn = (prm["conv_b_w"], prm["conv_b_b"])
    hf, st_f = _lru_scan(proj, *rnn, prm["wcat"][0], prm["ba"][0], prm["bx"][0], prm["lam"][0], h0[0],
                         reverse=False, width=width)
    if not need_output:
        _, st_b = _lru_scan(proj, *rnn, prm["wcat"][1], prm["ba"][1], prm["bx"][1], prm["lam"][1], h0[1],
                            reverse=True, width=width)
        return None, (st_f, st_b)
    zb, st_b = _lru_scan(proj, *rnn, prm["wcat"][1], prm["ba"][1], prm["bx"][1], prm["lam"][1], h0[1],
                         reverse=True, width=width, hf=hf)
    merged = _merge(proj, zb, prm["conv_a_w"], prm["conv_a_b"], prm["w_pa"], prm["w_pb"], width=width)
    return _proj_residual(merged, prm["w_o"], x, g1), (st_f, st_b)


def _peer(x, mods, prm, final_g, *, final_norm=False):
    sh2, sc2, g2 = mods
    bsz, t, d = x.shape
    n = bsz * t
    xf = x.reshape(1, n, d)
    assert ROW_TILE % TOKEN_BLOCK == 0 and TOKEN_BLOCK % SC_WORKERS == 0
    n_sc = int(n * SC_SHARE) // ROW_TILE * ROW_TILE
    n_tc = n - n_sc

    def route(start, count):
        q, h = _norm_mod_matmul_rows(xf, prm["norm2_g"], sh2, sc2, prm["wq"],
                                     start=start, count=count, per_batch=t)
        e, w = _route(q, prm["k1"], prm["k2"])
        return e, w, h

    if n_sc:
        e, w, h = route(n_tc, n_sc)
        peer_sc = _experts_sc(e[0], w[0], h[0], prm["table"])
    e, w, h = route(0, n_tc)
    out = _experts(e, w, h, xf, g2, final_g, prm["table"], n_tok=n_tc, per_batch=t, final_norm=final_norm)
    if n_sc:
        out = _residual(xf, peer_sc, g2, final_g, out, start=n_tc, per_batch=t, final_norm=final_norm)
    return out.reshape(bsz, t, d)


def kernel(x, c, ctx, c_ctx, ada_w, ada_b, norm1_g, norm2_g, w_in, conv_a_w, conv_a_b, conv_b_w, conv_b_b,
           lru_wa, lru_ba, lru_wx, lru_bx, lru_lam, w_pa, w_pb, w_o, peer_wq, peer_k1, peer_k2, peer_u,
           peer_v, final_g):
    depth = ada_w.shape[0]
    bsz, _, d = x.shape
    ctx_len = ctx.shape[1]
    cwid = conv_a_w.shape[2]
    rw = conv_b_w.shape[2]
    assert rw == d and (2 * rw + 2 * d) % cwid == 0 and bsz + 1 <= 8

    cond = jnp.zeros((8, d), F32).at[:bsz].set(c).at[bsz].set(c_ctx)
    zero_state = jnp.zeros((2, bsz, 1, rw), F32)
    s3 = 3 * cwid

    for l in range(depth):
        w_in_l = w_in[l]
        prm = dict(
            norm1_g=norm1_g[l], norm2_g=norm2_g[l],
            w_in=jnp.concatenate([w_in_l[:, s3:], w_in_l[:, :s3]], axis=1).astype(BF16),
            conv_a_w=conv_a_w[l], conv_a_b=conv_a_b[l], conv_b_w=conv_b_w[l], conv_b_b=conv_b_b[l],
            wcat=jnp.concatenate([lru_wa[l], lru_wx[l]], axis=-1).astype(BF16),
            ba=lru_ba[l], bx=lru_bx[l], lam=lru_lam[l],
            w_pa=w_pa[l].astype(BF16), w_pb=w_pb[l].astype(BF16), w_o=w_o[l].astype(BF16),
            wq=peer_wq[l].astype(BF16), k1=peer_k1[l].astype(BF16), k2=peer_k2[l].astype(BF16),
            table=_pack_tables(peer_u[l], peer_v[l]),
        )
        m = _adaln(cond, ada_w[l], ada_b[l])
        lat = [m[:bsz, k * d:(k + 1) * d].reshape(bsz, 1, d) for k in range(6)]
        cmod = [jnp.broadcast_to(m[bsz, k * d:(k + 1) * d].reshape(1, 1, d), (bsz, 1, d)) for k in range(6)]
        last = l == depth - 1

        ctx_mix, ctx_state = _mixer(ctx, cmod[0:3], ctx_len, zero_state, prm, need_output=not last)
        if not last:
            ctx = _peer(ctx_mix, cmod[3:6], prm, final_g)
        x = _mixer(x, lat[0:3], GRID_W, ctx_state, prm)[0]
        x = _peer(x, lat[3:6], prm, final_g, final_norm=last)
    return x
```

```python
import functools

import jax
import jax.numpy as jnp
from jax import lax
from jax.experimental import pallas as pl
from jax.experimental.pallas import tpu as pltpu
from jax.experimental.pallas import tpu_sc as plsc

GRID_W = 64
TOPK = 16
LRU_C = 8.0
EPS = 1e-6
VMEM_LIMIT_BYTES = 56 * 1024 * 1024
LANES = 128

F32 = jnp.float32
BF16 = jnp.bfloat16
U32 = jnp.uint32
I32 = jnp.int32


def _params(*sem):
    return pltpu.CompilerParams(dimension_semantics=sem, vmem_limit_bytes=VMEM_LIMIT_BYTES)


def _resident(shape):
    nd = len(shape)
    return pl.BlockSpec(shape, lambda *_: (0,) * nd, pipeline_mode=pl.Buffered(1))


def _tile(n, target):
    if n <= target:
        return n
    t = target - target % LANES
    while n % t:
        t -= LANES
    return t


def _gelu(x):
    return 0.5 * x * (1.0 + jnp.tanh(0.7978845608028654 * (x + 0.044715 * x * x * x)))


def _sigmoid(x):
    return 1.0 / (1.0 + jnp.exp(-x))


def _adaln_kernel(cond_ref, w_ref, b_ref, o_ref):
    cnd = cond_ref[...]
    s = (cnd * _sigmoid(cnd)).astype(BF16)
    o_ref[...] = jnp.dot(s, w_ref[...].astype(BF16), preferred_element_type=F32) + b_ref[...]


def _adaln(cond, w, b):
    rows, d = cond.shape
    n = w.shape[1]
    tn = _tile(n, 1024)
    return pl.pallas_call(
        _adaln_kernel,
        grid=(n // tn,),
        in_specs=[pl.BlockSpec((rows, d), lambda j: (0, 0)),
                  pl.BlockSpec((d, tn), lambda j: (0, j)),
                  pl.BlockSpec((1, tn), lambda j: (0, j))],
        out_specs=pl.BlockSpec((rows, tn), lambda j: (0, j)),
        out_shape=jax.ShapeDtypeStruct((rows, n), F32),
        compiler_params=_params("arbitrary"),
        name="adaln",
    )(cond, w, b.reshape(1, n))


def _nmm_kernel(x_ref, g_ref, sh_ref, sc_ref, w_ref, o_ref, *rest, emit_h):
    if emit_h:
        h_out_ref, h_scr = rest
    else:
        (h_scr,) = rest

    @pl.when(pl.program_id(2) == 0)
    def _():
        x = x_ref[0]
        ms = jnp.mean(x * x, axis=-1, keepdims=True)
        y = x * lax.rsqrt(ms + EPS) * g_ref[...]
        h = y * (1.0 + sc_ref[0]) + sh_ref[0]
        h_scr[...] = h.astype(BF16)
        if emit_h:
            h_out_ref[0] = h

    o_ref[0] = jnp.dot(h_scr[...], w_ref[...], preferred_element_type=F32)


def _norm_mod_matmul(x, g, shift, scale, w, *, emit_h=False):
    bsz, t, d = x.shape
    n = w.shape[1]
    tm = min(512 if emit_h else 1024, t)
    tn = _tile(n, 1024)
    out_shape = [jax.ShapeDtypeStruct((bsz, t, n), F32)]
    out_specs = [pl.BlockSpec((1, tm, tn), lambda b, i, j: (b, i, j))]
    if emit_h:
        out_shape.append(jax.ShapeDtypeStruct((bsz, t, d), F32))
        out_specs.append(pl.BlockSpec((1, tm, d), lambda b, i, j: (b, i, 0)))
    res = pl.pallas_call(
        functools.partial(_nmm_kernel, emit_h=emit_h),
        grid=(bsz, t // tm, n // tn),
        in_specs=[pl.BlockSpec((1, tm, d), lambda b, i, j: (b, i, 0)),
                  pl.BlockSpec((1, d), lambda b, i, j: (0, 0)),
                  pl.BlockSpec((1, 1, d), lambda b, i, j: (b, 0, 0)),
                  pl.BlockSpec((1, 1, d), lambda b, i, j: (b, 0, 0)),
                  pl.BlockSpec((d, tn), lambda b, i, j: (0, j))],
        out_specs=out_specs,
        out_shape=out_shape,
        scratch_shapes=[pltpu.VMEM((tm, d), BF16)],
        compiler_params=_params("arbitrary", "arbitrary", "arbitrary"),
        name="norm_mod_matmul",
    )(x, g.reshape(1, d), shift, scale, w)
    return res if emit_h else res[0]


ROW_TILE = 512


def _norm_mod_matmul_rows(xf, g, shift, scale, w, *, start, count, per_batch):
    d = xf.shape[2]
    n = w.shape[1]
    tm = min(ROW_TILE, per_batch)
    tn = _tile(n, 1024)
    assert start % tm == 0 and count % tm == 0 and per_batch % tm == 0
    first, per = start // tm, per_batch // tm
    return pl.pallas_call(
        functools.partial(_nmm_kernel, emit_h=True),
        grid=(1, count // tm, n // tn),
        in_specs=[pl.BlockSpec((1, tm, d), lambda b, i, j: (0, first + i, 0)),
                  pl.BlockSpec((1, d), lambda b, i, j: (0, 0)),
                  pl.BlockSpec((1, 1, d), lambda b, i, j: ((first + i) // per, 0, 0)),
                  pl.BlockSpec((1, 1, d), lambda b, i, j: ((first + i) // per, 0, 0)),
                  pl.BlockSpec((d, tn), lambda b, i, j: (0, j))],
        out_specs=[pl.BlockSpec((1, tm, tn), lambda b, i, j: (0, i, j)),
                   pl.BlockSpec((1, tm, d), lambda b, i, j: (0, i, 0))],
        out_shape=[jax.ShapeDtypeStruct((1, count, n), F32), jax.ShapeDtypeStruct((1, count, d), F32)],
        scratch_shapes=[pltpu.VMEM((tm, d), BF16)],
        compiler_params=_params("arbitrary", "arbitrary", "arbitrary"),
        name="norm_mod_matmul",
    )(xf, g.reshape(1, d), shift, scale, w)


def _row_shift(x, off, pos, width):
    if off == 0:
        return x
    n = x.shape[0]
    r = pltpu.roll(x, (-off) % n, axis=0)
    valid = jnp.logical_and(pos + off >= 0, pos + off < width)
    return jnp.where(valid, r, 0.0)


def _tile_scan(a, u, state, reverse):
    tm, c = a.shape
    ngrp = tm // SUBLANES
    a3 = a.reshape(ngrp, SUBLANES, c)
    u3 = u.reshape(ngrp, SUBLANES, c)
    sub = lax.broadcasted_iota(I32, (1, SUBLANES, 1), 1)
    for d in (1, 2, 4):
        shift = SUBLANES - d if reverse else d
        keep = sub < SUBLANES - d if reverse else sub >= d
        a_s = pltpu.roll(a3, shift, axis=1)
        u_s = pltpu.roll(u3, shift, axis=1)
        u3 = jnp.where(keep, u3 + a3 * u_s, u3)
        a3 = jnp.where(keep, a3 * a_s, a3)
    hs = [None] * ngrp
    for g in (range(ngrp - 1, -1, -1) if reverse else range(ngrp)):
        hg = u3[g] + a3[g] * state
        hs[g] = hg
        state = hg[0:1, :] if reverse else hg[SUBLANES - 1:SUBLANES, :]
    return jnp.concatenate(hs, axis=0), state


def _scan_kernel(*refs, reverse, width, final, nblk):
    if final:
        (xr_ref, cw_ref, cb_ref, w_ref, ba_ref, bx_ref, lam_ref, h0_ref, hf_ref, gr_ref,
         o_ref, st_ref, carry) = refs
    else:
        (xr_ref, cw_ref, cb_ref, w_ref, ba_ref, bx_ref, lam_ref, h0_ref,
         o_ref, st_ref, carry) = refs

    @pl.when(pl.program_id(1) == 0)
    def _():
        carry[...] = h0_ref[0]

    x = xr_ref[0]
    tm, rw = x.shape
    bs = rw // nblk
    row = lax.broadcasted_iota(I32, (tm, 1), 0)
    pos = row % width
    cw = cw_ref[...]
    xc = cb_ref[...] + sum(cw[k:k + 1, :] * _row_shift(x, k - 2, pos, width) for k in range(4))

    for n in range(nblk):
        cols = slice(n * bs, (n + 1) * bs)
        xn = xc[:, cols]
        pre = jnp.dot(xn.astype(BF16), w_ref[n], preferred_element_type=F32)
        r = _sigmoid(pre[:, :bs] + ba_ref[:, cols])
        ig = _sigmoid(pre[:, bs:] + bx_ref[:, cols])
        z = -lam_ref[:, cols]
        sp = jnp.maximum(z, 0.0) + jnp.log(1.0 + jnp.exp(-jnp.abs(z)))
        log_a = -LRU_C * r * sp
        a = jnp.exp(log_a)
        u = jnp.sqrt(1.0 - a * a) * (ig * xn)
        h, last = _tile_scan(a, u, carry[:, cols], reverse)
        carry[:, cols] = last
        if final:
            zb = (hf_ref[0, :, cols] + h) * _gelu(gr_ref[0, :, cols])
            o_ref[0, :, cols] = zb.astype(o_ref.dtype)
        else:
            o_ref[0, :, cols] = h
    st_ref[0] = carry[...]


def _lru_scan(proj, conv_w, conv_b, wcat, ba, bx, lam, h0, *, reverse, width, hf=None):
    bsz, t, _ = proj.shape
    rw = conv_w.shape[1]
    nblk = wcat.shape[0]
    final = hf is not None
    tm = min(256, t)
    nt = t // tm
    tix = (lambda i: nt - 1 - i) if reverse else (lambda i: i)
    vec = lambda a: a.reshape(1, rw)
    in_specs = [pl.BlockSpec((1, tm, rw), lambda b, i: (b, tix(i), 0)),
                _resident((4, rw)), _resident((1, rw)), _resident(wcat.shape),
                _resident((1, rw)), _resident((1, rw)), _resident((1, rw)),
                pl.BlockSpec((1, 1, rw), lambda b, i: (b, 0, 0))]
    args = [proj, conv_w, vec(conv_b), wcat, vec(ba), vec(bx), vec(lam), h0]
    if final:
        in_specs += [pl.BlockSpec((1, tm, rw), lambda b, i: (b, tix(i), 0)),
                     pl.BlockSpec((1, tm, rw), lambda b, i: (b, tix(i), 1))]
        args += [hf, proj]
    out, state = pl.pallas_call(
        functools.partial(_scan_kernel, reverse=reverse, width=width, final=final, nblk=nblk),
        grid=(bsz, nt),
        in_specs=in_specs,
        out_specs=[pl.BlockSpec((1, tm, rw), lambda b, i: (b, tix(i), 0)),
                   pl.BlockSpec((1, 1, rw), lambda b, i: (b, 0, 0))],
        out_shape=[jax.ShapeDtypeStruct((bsz, t, rw), BF16 if final else F32),
                   jax.ShapeDtypeStruct((bsz, 1, rw), F32)],
        scratch_shapes=[pltpu.VMEM((1, rw), F32)],
        compiler_params=_params("arbitrary", "arbitrary"),
        name="lru_scan_bwd" if reverse else "lru_scan_fwd",
    )(*args)
    return out, state


def _merge_kernel(ga_ref, gb_ref, ba_ref, ca_ref, va_ref, zb_ref, cw_ref, cb_ref, wpa_ref, wpb_ref,
                  o_ref, *, width):
    cv = ca_ref[0] * va_ref[0]
    tm = cv.shape[0]
    pos = lax.broadcasted_iota(I32, (tm, 1), 0) % width
    cw = cw_ref[...]
    conv = cb_ref[...] + sum(cw[k:k + 1, :] * _row_shift(cv, k - 1, pos, width) for k in range(3))
    za = (ba_ref[0] * conv).astype(BF16)
    ya = jnp.dot(za, wpa_ref[...], preferred_element_type=F32)
    yb = jnp.dot(zb_ref[0], wpb_ref[...], preferred_element_type=F32)
    o_ref[0] = (_sigmoid(ga_ref[0]) * ya + _sigmoid(gb_ref[0]) * yb).astype(o_ref.dtype)


def _merge(proj, zb, conv_w, conv_b, w_pa, w_pb, *, width):
    bsz, t, _ = proj.shape
    cwid, d = w_pa.shape
    rw = w_pb.shape[0]
    tm = min(256, t)
    ga_blk = (2 * rw) // d
    ca_blk = (2 * rw + 2 * d) // cwid
    pblk = lambda wdt, k: pl.BlockSpec((1, tm, wdt), lambda b, i: (b, i, k))
    return pl.pallas_call(
        functools.partial(_merge_kernel, width=width),
        grid=(bsz, t // tm),
        in_specs=[pblk(d, ga_blk), pblk(d, ga_blk + 1),
                  pblk(cwid, ca_blk), pblk(cwid, ca_blk + 1), pblk(cwid, ca_blk + 2),
                  pl.BlockSpec((1, tm, rw), lambda b, i: (b, i, 0)),
                  _resident((3, cwid)), _resident((1, cwid)),
                  _resident((cwid, d)), _resident((rw, d))],
        out_specs=pl.BlockSpec((1, tm, d), lambda b, i: (b, i, 0)),
        out_shape=jax.ShapeDtypeStruct((bsz, t, d), BF16),
        compiler_params=_params("arbitrary", "arbitrary"),
        name="merge",
    )(proj, proj, proj, proj, proj, zb, conv_w, conv_b.reshape(1, cwid), w_pa, w_pb)


def _proj_res_kernel(m_ref, w_ref, x_ref, g_ref, o_ref):
    y = jnp.dot(m_ref[0], w_ref[...], preferred_element_type=F32)
    o_ref[0] = x_ref[0] + g_ref[0] * y


def _proj_residual(m, w, x, gate):
    bsz, t, d = x.shape
    k = m.shape[2]
    tm = min(512, t)
    return pl.pallas_call(
        _proj_res_kernel,
        grid=(bsz, t // tm),
        in_specs=[pl.BlockSpec((1, tm, k), lambda b, i: (b, i, 0)),
                  _resident((k, d)),
                  pl.BlockSpec((1, tm, d), lambda b, i: (b, i, 0)),
                  pl.BlockSpec((1, 1, d), lambda b, i: (b, 0, 0))],
        out_specs=pl.BlockSpec((1, tm, d), lambda b, i: (b, i, 0)),
        out_shape=jax.ShapeDtypeStruct((bsz, t, d), F32),
        compiler_params=_params("arbitrary", "arbitrary"),
        name="proj_residual",
    )(m, w, x, gate)


def _top16(s, payload=None):
    iota = lax.broadcasted_iota(I32, s.shape, 0).astype(F32)
    vals, picks = [], []
    for _ in range(TOPK):
        m = jnp.max(s, axis=0, keepdims=True)
        idx = jnp.min(jnp.where(s == m, iota, float(s.shape[0])), axis=0, keepdims=True)
        hit = iota == idx
        vals.append(m)
        if payload is None:
            picks.append(idx)
        else:
            picks.append(jnp.sum(jnp.where(hit, payload, 0.0), axis=0, keepdims=True))
        s = jnp.where(hit, -jnp.inf, s)
    return jnp.concatenate(vals, axis=0), jnp.concatenate(picks, axis=0)


def _pair_candidates(v1, i1, v2, i2, keys):
    sub = 8
    row = lax.broadcasted_iota(I32, (sub, 1), 0)
    cand = [v1[0:1] + v2]
    ecand = [i1[0:1] * keys + i2]
    for a in range(1, sub):
        ok = row < TOPK // (a + 1)
        cand.append(jnp.where(ok, v1[a:a + 1] + v2[0:sub], -jnp.inf))
        ecand.append(i1[a:a + 1] * keys + i2[0:sub])
    cand.append(v1[sub:] + v2[0:1])
    ecand.append(i1[sub:] * keys + i2[0:1])
    return jnp.concatenate(cand, axis=0), jnp.concatenate(ecand, axis=0)


def _route_kernel(q_ref, k1_ref, k2_ref, e_ref, w_ref, *, heads, keys):
    q = q_ref[0].astype(BF16)
    half = k1_ref.shape[2]
    dn = (((1,), (1,)), ((), ()))
    es, ws = [], []
    for h in range(heads):
        q1 = q[:, h * 2 * half: h * 2 * half + half]
        q2 = q[:, h * 2 * half + half: (h + 1) * 2 * half]
        s1 = lax.dot_general(k1_ref[h], q1, dn, preferred_element_type=F32)
        s2 = lax.dot_general(k2_ref[h], q2, dn, preferred_element_type=F32)
        v1, i1 = _top16(s1)
        v2, i2 = _top16(s2)
        score, e = _top16(*_pair_candidates(v1, i1, v2, i2, float(keys)))
        p = jnp.exp(score - score[0:1, :])
        ws.append(p / jnp.sum(p, axis=0, keepdims=True))
        es.append(e)
    e_ref[0] = jnp.concatenate(es, axis=0).T.astype(I32)
    w_ref[0] = jnp.concatenate(ws, axis=0).T


def _route(q, k1, k2):
    bsz, t, hq = q.shape
    heads, keys, _ = k1.shape
    tb = min(256, t)
    nsel = heads * TOPK
    return pl.pallas_call(
        functools.partial(_route_kernel, heads=heads, keys=keys),
        grid=(bsz, t // tb),
        in_specs=[pl.BlockSpec((1, tb, hq), lambda b, i: (b, i, 0)),
                  _resident(k1.shape), _resident(k2.shape)],
        out_specs=[pl.BlockSpec((1, tb, nsel), lambda b, i: (b, i, 0)),
                   pl.BlockSpec((1, tb, nsel), lambda b, i: (b, i, 0))],
        out_shape=[jax.ShapeDtypeStruct((bsz, t, nsel), I32),
                   jax.ShapeDtypeStruct((bsz, t, nsel), F32)],
        compiler_params=_params("arbitrary", "arbitrary"),
        name="peer_route",
    )(q, k1, k2)


def _pack_kernel(u_ref, v_ref, o_ref):
    half = u_ref.shape[1] // 2

    def pack(x):
        bits = lax.bitcast_convert_type(x.astype(BF16).astype(F32), U32)
        return (bits[:, :half] >> 16) | (bits[:, half:] & jnp.uint32(0xFFFF0000))

    o_ref[:, :half] = pack(u_ref[...])
    o_ref[:, half:] = pack(v_ref[...])


def _pack_tables(u, v):
    e, d = u.shape
    te = 256
    return pl.pallas_call(
        _pack_kernel,
        grid=(e // te,),
        in_specs=[pl.BlockSpec((te, d), lambda i: (i, 0)), pl.BlockSpec((te, d), lambda i: (i, 0))],
        out_specs=pl.BlockSpec((te, d), lambda i: (i, 0)),
        out_shape=jax.ShapeDtypeStruct((e, d), U32),
        compiler_params=_params("arbitrary"),
        name="pack_tables",
    )(u, v)


def _unpack(words):
    lo = lax.bitcast_convert_type(words << 16, F32)
    hi = lax.bitcast_convert_type(words & jnp.uint32(0xFFFF0000), F32)
    return lo, hi


NSLOT = 4
LOOKAHEAD = NSLOT - 1
SUBLANES = 8


def _expert_kernel(e_ref, w_ref, h_ref, x_ref, g_ref, fg_ref, tab_ref, o_ref, *scratch, final_norm):
    bufs, sems = scratch[:NSLOT], scratch[NSLOT]
    tb, nsel = w_ref.shape[1], w_ref.shape[2]
    d = h_ref.shape[2]
    half = d // 2
    ngrp = nsel // SUBLANES

    def row_copy(t, j, slot):
        return pltpu.make_async_copy(tab_ref.at[pl.ds(e_ref[0, t, j], 1), :],
                                     bufs[slot].at[pl.ds(j, 1), :], sems.at[slot])

    lane = lax.broadcasted_iota(I32, (SUBLANES, nsel), 1)
    sub = lax.broadcasted_iota(I32, (SUBLANES, nsel), 0)

    def token(t, slot, t_next):
        for j in range(nsel):
            row_copy(t, j, slot).wait()
        buf = bufs[slot]
        nslot = (slot + LOOKAHEAD) % NSLOT
        hrow = h_ref[0, pl.ds(t, 1), :]
        xlo = jnp.broadcast_to(hrow[:, :half], (SUBLANES, half))
        xhi = jnp.broadcast_to(hrow[:, half:], (SUBLANES, half))
        wrow = w_ref[0, pl.ds(t, 1), :]
        acc_lo = jnp.zeros((SUBLANES, half), F32)
        acc_hi = jnp.zeros((SUBLANES, half), F32)
        for g in range(ngrp):
            rows = slice(g * SUBLANES, (g + 1) * SUBLANES)
            ulo, uhi = _unpack(buf[rows, :half])
            act = jnp.sum(ulo * xlo + uhi * xhi, axis=1, keepdims=True)
            wcol = jnp.sum(jnp.where(lane == sub + g * SUBLANES, wrow, 0.0), axis=1, keepdims=True)
            coef = wcol * _gelu(act)
            vlo, vhi = _unpack(buf[rows, half:])
            acc_lo = acc_lo + coef * vlo
            acc_hi = acc_hi + coef * vhi
            if t_next is not None:
                for j in range(g * SUBLANES, (g + 1) * SUBLANES):
                    row_copy(t_next, j, nslot).start()
        gate = g_ref[0]
        olo = jnp.sum(acc_lo, axis=0, keepdims=True)
        ohi = jnp.sum(acc_hi, axis=0, keepdims=True)
        o_ref[0, pl.ds(t, 1), :half] = x_ref[0, pl.ds(t, 1), :half] + gate[:, :half] * olo
        o_ref[0, pl.ds(t, 1), half:] = x_ref[0, pl.ds(t, 1), half:] + gate[:, half:] * ohi

    for t in range(LOOKAHEAD):
        for j in range(nsel):
            row_copy(t, j, t).start()

    def body(q, c):
        for s in range(NSLOT):
            t = q * NSLOT + s
            token(t, s, t + LOOKAHEAD)
        return c

    lax.fori_loop(0, tb // NSLOT - 1, body, 0)
    for s in range(NSLOT):
        t = tb - NSLOT + s
        token(t, s, t + LOOKAHEAD if s == 0 else None)

    if final_norm:
        y = o_ref[0]
        ms = jnp.mean(y * y, axis=-1, keepdims=True)
        o_ref[0] = y * lax.rsqrt(ms + EPS) * fg_ref[...]


TOKEN_BLOCK = 256


def _experts(e, w, h, x, gate, final_g, table, *, n_tok, per_batch, final_norm):
    d = x.shape[2]
    nsel = e.shape[2]
    tb = TOKEN_BLOCK
    assert tb % NSLOT == 0 and nsel % SUBLANES == 0 and n_tok % tb == 0 and per_batch % tb == 0
    tok = lambda wdt: pl.BlockSpec((1, tb, wdt), lambda i: (0, i, 0))
    return pl.pallas_call(
        functools.partial(_expert_kernel, final_norm=final_norm),
        grid=(n_tok // tb,),
        in_specs=[pl.BlockSpec((1, tb, nsel), lambda i: (0, i, 0), memory_space=pltpu.SMEM),
                  tok(nsel), tok(d), tok(d),
                  pl.BlockSpec((1, 1, d), lambda i: (i // (per_batch // tb), 0, 0)),
                  pl.BlockSpec((1, d), lambda i: (0, 0)),
                  pl.BlockSpec(memory_space=pl.ANY)],
        out_specs=tok(d),
        out_shape=jax.ShapeDtypeStruct(x.shape, F32),
        scratch_shapes=[pltpu.VMEM((nsel, d), U32)] * NSLOT + [pltpu.SemaphoreType.DMA((NSLOT,))],
        compiler_params=_params("arbitrary"),
        name="peer_experts",
    )(e, w, h, x, gate, final_g.reshape(1, d), table)


SC_LANES = 16
SC_WORKERS = 32
SC_SHARE = 0.5625
SC_LEAD = 2048


def _sc_expert_kernel(e_hbm, w_hbm, h_hbm, tab_hbm, o_hbm, idx_v, w_v, h_v, out_v, rows_a, rows_b,
                      sem_a, sem_b, sem_i0, sem_i1, sem_o0, sem_o1, *, per_worker):
    lanes = SC_LANES
    nsel = idx_v.shape[1]
    d = h_v.shape[1]
    half = d // 2
    npair = nsel // lanes // 2
    base = (lax.axis_index("s") * 2 + lax.axis_index("c")) * per_worker
    lane = lax.iota(I32, lanes)
    bufs = ((rows_a, sem_a), (rows_b, sem_b))
    sem_in = (sem_i0, sem_i1)
    sem_out = (sem_o0, sem_o1)

    def in_copies(t, p):
        return (pltpu.make_async_copy(e_hbm.at[t], idx_v.at[p], sem_in[p]),
                pltpu.make_async_copy(w_hbm.at[t], w_v.at[p], sem_in[p]),
                pltpu.make_async_copy(h_hbm.at[t], h_v.at[p], sem_in[p]))

    def out_copy(t, p):
        return pltpu.make_async_copy(out_v.at[p], o_hbm.at[t], sem_out[p])

    def gather(p, c, b):
        rows, sem = bufs[b]
        return pltpu.make_async_copy(tab_hbm.at[idx_v.at[p, pl.ds(c * lanes, lanes)]], rows, sem)

    def compute(p, c, rows, first):
        w16 = w_v[p, pl.ds(c * lanes, lanes)]

        def ublock(kb, accs):
            off = pl.multiple_of(kb * lanes, lanes)
            hl = h_v[p, pl.ds(off, lanes)]
            hh = h_v[p, pl.ds(half + off, lanes)]
            out = []
            for r in range(lanes):
                lo, hi = _unpack(rows[r, pl.ds(off, lanes)])
                out.append(accs[r] + lo * hl + hi * hh)
            return tuple(out)

        zeros = tuple(jnp.zeros((lanes,), F32) for _ in range(lanes))
        accs = plsc.parallel_loop(0, half // lanes, carry=zeros)(ublock)
        act = jnp.zeros((lanes,), F32)
        for r in range(lanes):
            act = jnp.where(lane == r, jnp.sum(accs[r]), act)
        z = 0.7978845608028654 * (act + 0.044715 * act * act * act)
        th = 1.0 - 2.0 / (1.0 + jnp.exp(2.0 * z))
        coef16 = w16 * (0.5 * act * (1.0 + th))
        coefs = [jnp.take(coef16, jnp.full((lanes,), r, I32)) for r in range(lanes)]

        def vblock(kb, carry):
            off = pl.multiple_of(kb * lanes, lanes)
            if first:
                al = jnp.zeros((lanes,), F32)
                ah = jnp.zeros((lanes,), F32)
            else:
                al = out_v[p, pl.ds(off, lanes)]
                ah = out_v[p, pl.ds(half + off, lanes)]
            for r in range(lanes):
                lo, hi = _unpack(rows[r, pl.ds(half + off, lanes)])
                al = al + coefs[r] * lo
                ah = ah + coefs[r] * hi
            out_v[p, pl.ds(off, lanes)] = al
            out_v[p, pl.ds(half + off, lanes)] = ah
            return carry

        plsc.parallel_loop(0, half // lanes, carry=jnp.int32(0))(vblock)

    for cp in in_copies(base, 0):
        cp.start()
    for cp in in_copies(base, 0):
        cp.wait()
    gather(0, 0, 0).start()

    @pl.loop(0, per_worker // 2)
    def _(i2):
        for p in range(2):
            i = i2 * 2 + p
            t = base + i
            more = i + 1 < per_worker

            @pl.when(more)
            def _():
                for cp in in_copies(t + 1, 1 - p):
                    cp.start()

            @pl.when(i >= 2)
            def _():
                out_copy(t - 2, p).wait()

            gather(p, 1, 1).start()
            gather(p, 0, 0).wait()
            compute(p, 0, rows_a, True)
            gather(p, 2, 0).start()
            gather(p, 1, 1).wait()
            compute(p, 1, rows_b, False)

            @pl.loop(1, npair)
            def _(c2):
                c = c2 * 2
                gather(p, c + 1, 1).start()
                gather(p, c, 0).wait()
                compute(p, c, rows_a, False)

                @pl.when(c2 < npair - 1)
                def _():
                    gather(p, c + 2, 0).start()

                @pl.when(jnp.logical_and(c2 == npair - 1, more))
                def _():
                    for cp in in_copies(t + 1, 1 - p):
                        cp.wait()
                    gather(1 - p, 0, 0).start()

                gather(p, c + 1, 1).wait()
                compute(p, c + 1, rows_b, False)

            out_copy(t, p).start()

    out_copy(base + per_worker - 2, 0).wait()
    out_copy(base + per_worker - 1, 1).wait()


def _experts_sc(e, w, h, table):
    count, nsel = e.shape
    d = h.shape[1]
    assert count % (2 * SC_WORKERS) == 0 and nsel // SC_LANES >= 4 and nsel % (2 * SC_LANES) == 0
    mesh = plsc.VectorSubcoreMesh(core_axis_name="c", subcore_axis_name="s")
    dma = pltpu.SemaphoreType.DMA
    return pl.kernel(
        functools.partial(_sc_expert_kernel, per_worker=count // SC_WORKERS),
        mesh=mesh,
        compiler_params=pltpu.CompilerParams(needs_layout_passes=False),
        out_type=jax.ShapeDtypeStruct((count, d), F32),
        scratch_types=[pltpu.VMEM((2, nsel), I32), pltpu.VMEM((2, nsel), F32),
                       pltpu.VMEM((2, d), F32), pltpu.VMEM((2, d), F32),
                       pltpu.VMEM((SC_LANES, d), U32), pltpu.VMEM((SC_LANES, d), U32),
                       dma, dma, dma, dma, dma, dma],
    )(e, w, h, table)


def _residual_kernel(x_ref, p_ref, g_ref, fg_ref, base_ref, o_ref, *, final_norm):
    del base_ref
    y = x_ref[0] + g_ref[0] * p_ref[...]
    if final_norm:
        ms = jnp.mean(y * y, axis=-1, keepdims=True)
        y = y * lax.rsqrt(ms + EPS) * fg_ref[...]
    o_ref[0] = y


def _residual(x, peer, gate, final_g, base, *, start, per_batch, final_norm):
    count, d = peer.shape
    tb = TOKEN_BLOCK
    assert count % tb == 0 and start % tb == 0
    first = start // tb
    return pl.pallas_call(
        functools.partial(_residual_kernel, final_norm=final_norm),
        grid=(count // tb,),
        in_specs=[pl.BlockSpec((1, tb, d), lambda i: (0, first + i, 0)),
                  pl.BlockSpec((tb, d), lambda i: (i, 0)),
                  pl.BlockSpec((1, 1, d), lambda i: ((first + i) // (per_batch // tb), 0, 0)),
                  pl.BlockSpec((1, d), lambda i: (0, 0)),
                  pl.BlockSpec(memory_space=pl.ANY)],
        out_specs=pl.BlockSpec((1, tb, d), lambda i: (0, first + i, 0)),
        out_shape=jax.ShapeDtypeStruct(base.shape, F32),
        input_output_aliases={4: 0},
        compiler_params=_params("arbitrary"),
        name="peer_residual",
    )(x, peer, gate, final_g.reshape(1, d), base)


def _mixer(x, mods, width, h0, prm, *, need_output=True):
    sh1, sc1, g1 = mods
    proj = _norm_mod_matmul(x, prm["norm1_g"], sh1, sc1, prm["w_in"])
    rnn = (prm["conv_b_w"], prm["conv_b_b"])
    hf, st_f = _lru_scan(proj, *rnn, prm["wcat"][0], prm["ba"][0], prm["bx"][0], prm["lam"][0], h0[0],
                         reverse=False, width=width)
    if not need_output:
        _, st_b = _lru_scan(proj, *rnn, prm["wcat"][1], prm["ba"][1], prm["bx"][1], prm["lam"][1], h0[1],
                            reverse=True, width=width)
        return None, (st_f, st_b)
    zb, st_b = _lru_scan(proj, *rnn, prm["wcat"][1], prm["ba"][1], prm["bx"][1], prm["lam"][1], h0[1],
                         reverse=True, width=width, hf=hf)
    merged = _merge(proj, zb, prm["conv_a_w"], prm["conv_a_b"], prm["w_pa"], prm["w_pb"], width=width)
    return _proj_residual(merged, prm["w_o"], x, g1), (st_f, st_b)


def _peer(x, mods, prm, final_g, *, final_norm=False):
    sh2, sc2, g2 = mods
    bsz, t, d = x.shape
    n = bsz * t
    xf = x.reshape(1, n, d)
    assert ROW_TILE % TOKEN_BLOCK == 0 and TOKEN_BLOCK % SC_WORKERS == 0
    n_sc = int(n * SC_SHARE) // ROW_TILE * ROW_TILE
    n_tc = n - n_sc

    def route(start, count):
        q, h = _norm_mod_matmul_rows(xf, prm["norm2_g"], sh2, sc2, prm["wq"],
                                     start=start, count=count, per_batch=t)
        e, w = _route(q, prm["k1"], prm["k2"])
        return e, w, h

    lead = min(SC_LEAD, n_sc)
    pieces = [(n - lead, lead), (n_tc, n_sc - lead)] if n_sc > lead else [(n_tc, n_sc)] if n_sc else []
    sums = []
    for start, count in pieces:
        e, w, h = route(start, count)
        sums.append(_experts_sc(e[0], w[0], h[0], prm["table"]))
    e, w, h = route(0, n_tc)
    out = _experts(e, w, h, xf, g2, final_g, prm["table"], n_tok=n_tc, per_batch=t, final_norm=final_norm)
    for (start, _), peer_sc in zip(pieces, sums):
        out = _residual(xf, peer_sc, g2, final_g, out, start=start, per_batch=t, final_norm=final_norm)
    return out.reshape(bsz, t, d)


def kernel(x, c, ctx, c_ctx, ada_w, ada_b, norm1_g, norm2_g, w_in, conv_a_w, conv_a_b, conv_b_w, conv_b_b,
           lru_wa, lru_ba, lru_wx, lru_bx, lru_lam, w_pa, w_pb, w_o, peer_wq, peer_k1, peer_k2, peer_u,
           peer_v, final_g):
    depth = ada_w.shape[0]
    bsz, _, d = x.shape
    ctx_len = ctx.shape[1]
    cwid = conv_a_w.shape[2]
    rw = conv_b_w.shape[2]
    assert rw == d and (2 * rw + 2 * d) % cwid == 0 and bsz + 1 <= 8

    cond = jnp.zeros((8, d), F32).at[:bsz].set(c).at[bsz].set(c_ctx)
    zero_state = jnp.zeros((2, bsz, 1, rw), F32)
    s3 = 3 * cwid

    for l in range(depth):
        w_in_l = w_in[l]
        prm = dict(
            norm1_g=norm1_g[l], norm2_g=norm2_g[l],
            w_in=jnp.concatenate([w_in_l[:, s3:], w_in_l[:, :s3]], axis=1).astype(BF16),
            conv_a_w=conv_a_w[l], conv_a_b=conv_a_b[l], conv_b_w=conv_b_w[l], conv_b_b=conv_b_b[l],
            wcat=jnp.concatenate([lru_wa[l], lru_wx[l]], axis=-1).astype(BF16),
            ba=lru_ba[l], bx=lru_bx[l], lam=lru_lam[l],
            w_pa=w_pa[l].astype(BF16), w_pb=w_pb[l].astype(BF16), w_o=w_o[l].astype(BF16),
            wq=peer_wq[l].astype(BF16), k1=peer_k1[l].astype(BF16), k2=peer_k2[l].astype(BF16),
            table=_pack_tables(peer_u[l], peer_v[l]),
        )
        m = _adaln(cond, ada_w[l], ada_b[l])
        lat = [m[:bsz, k * d:(k + 1) * d].reshape(bsz, 1, d) for k in range(6)]
        cmod = [jnp.broadcast_to(m[bsz, k * d:(k + 1) * d].reshape(1, 1, d), (bsz, 1, d)) for k in range(6)]
        last = l == depth - 1

        ctx_mix, ctx_state = _mixer(ctx, cmod[0:3], ctx_len, zero_state, prm, need_output=not last)
        if not last:
            ctx = _peer(ctx_mix, cmod[3:6], prm, final_g)
        x = _mixer(x, lat[0:3], GRID_W, ctx_state, prm)[0]
        x = _peer(x, lat[3:6], prm, final_g, final_norm=last)
    return x
```
